```python
import math
import jax, jax.numpy as jnp
from jax import lax
import numpy as np

D_MODEL = 2048
BATCH = 4
SEQ = 4096
DEPTH = 1

CHUNK = 64
N_META = 16
PAD_LEAD = CHUNK - N_META
EPS = 1e-6

SSD_HEADS = 32
SSD_HEAD_DIM = 64
D_SSD = SSD_HEADS * SSD_HEAD_DIM
SSD_GROUPS = 8
SSD_HEADS_PER_GROUP = SSD_HEADS // SSD_GROUPS
D_STATE = 128
CONV_WIDTH = 4
D_CONV = D_SSD + 2 * SSD_GROUPS * D_STATE

ATT_Q_HEADS = 16
ATT_KV_HEADS = 4
ATT_REP = ATT_Q_HEADS // ATT_KV_HEADS
ATT_HEAD_DIM = 64
D_ATT = ATT_Q_HEADS * ATT_HEAD_DIM
D_KV = ATT_KV_HEADS * ATT_HEAD_DIM
WINDOW = 128
WINDOW_CHUNKS = WINDOW // CHUNK
BAND = (WINDOW_CHUNKS + 1) * CHUNK
ROPE_THETA = 10000.0

D_MIX = D_SSD + D_ATT
D_IN_PROJ = D_SSD + D_CONV + SSD_HEADS + D_ATT + 2 * D_KV + D_ATT

kernel_name = "hymba_ssd_swa_sink_streaming_layer"


def rmsnorm(x, w):
    x32 = x.astype(jnp.float32)
    y = x32 * lax.rsqrt(jnp.mean(x32 * x32, axis=-1, keepdims=True) + EPS)
    return (y * w.astype(jnp.float32)).astype(x.dtype)


def causal_depthwise_conv(u, w, b):
    out = lax.conv_general_dilated(
        u.astype(jnp.float32), w.astype(jnp.float32)[:, None, :],
        window_strides=(1,), padding=[(CONV_WIDTH - 1, 0)],
        dimension_numbers=("NWC", "WIO", "NWC"), feature_group_count=u.shape[-1])
    return out + b.astype(jnp.float32)


def ssd_chunked_scan(xs, dt, a, b_mat, c_mat):
    bsz, lp, g, r, p = xs.shape
    n = b_mat.shape[-1]
    nc = lp // CHUNK
    xs = xs.reshape(bsz, nc, CHUNK, g, r, p)
    dt = dt.reshape(bsz, nc, CHUNK, g, r)
    b_mat = b_mat.reshape(bsz, nc, CHUNK, g, n)
    c_mat = c_mat.reshape(bsz, nc, CHUNK, g, n)
    a_cs = jnp.cumsum(dt * a, axis=2)
    x_dt = xs * dt[..., None]
    causal = jnp.tril(jnp.ones((CHUNK, CHUNK), dtype=bool))[:, :, None, None]
    seg = a_cs[:, :, :, None] - a_cs[:, :, None, :]
    decay_ls = jnp.exp(jnp.where(causal, seg, -jnp.inf))
    cb = jnp.einsum("bclgn,bcsgn->bclsg", c_mat, b_mat)
    y_diag = jnp.einsum("bclsg,bclsgr,bcsgrp->bclgrp", cb, decay_ls, x_dt)
    decay_to_end = jnp.exp(a_cs[:, :, -1:] - a_cs)
    states = jnp.einsum("bclgn,bclgr,bclgrp->bcgrpn", b_mat, decay_to_end, x_dt)
    chunk_decay = jnp.exp(a_cs[:, :, -1])

    def step(h, inp):
        st, dec = inp
        return h * dec[..., None, None] + st, h

    h0 = jnp.zeros((bsz, g, r, p, n), xs.dtype)
    _, prev = lax.scan(step, h0, (jnp.moveaxis(states, 1, 0), jnp.moveaxis(chunk_decay, 1, 0)))
    prev = jnp.moveaxis(prev, 0, 1)
    y_off = jnp.einsum("bclgn,bcgrpn,bclgr->bclgrp", c_mat, prev, jnp.exp(a_cs))
    return (y_diag + y_off).reshape(bsz, lp, g, r, p)


def rope(t, pos):
    half = t.shape[-1] // 2
    inv = ROPE_THETA ** (-jnp.arange(half, dtype=jnp.float32) / half)
    ang = pos.astype(jnp.float32)[:, None] * inv[None, :]
    cos = jnp.cos(ang)[None, :, None, :]
    sin = jnp.sin(ang)[None, :, None, :]
    t1, t2 = t[..., :half], t[..., half:]
    return jnp.concatenate([t1 * cos - t2 * sin, t1 * sin + t2 * cos], axis=-1)


def banded_sink_attention(q, k, v, sinks):
    bsz, lp = q.shape[:2]
    nc = lp // CHUNK
    q = q.reshape(bsz, nc, CHUNK, ATT_KV_HEADS, ATT_REP, ATT_HEAD_DIM)
    k = k.reshape(bsz, nc, CHUNK, ATT_KV_HEADS, ATT_HEAD_DIM)
    v = v.reshape(bsz, nc, CHUNK, ATT_KV_HEADS, ATT_HEAD_DIM)
    padw = ((0, 0), (WINDOW_CHUNKS, 0), (0, 0), (0, 0), (0, 0))
    kp, vp = jnp.pad(k, padw), jnp.pad(v, padw)
    k_band = jnp.concatenate([kp[:, j:j + nc] for j in range(WINDOW_CHUNKS + 1)], axis=2)
    v_band = jnp.concatenate([vp[:, j:j + nc] for j in range(WINDOW_CHUNKS + 1)], axis=2)
    scale = ATT_HEAD_DIM ** -0.5
    s = jnp.einsum("bclhrd,bcshd->bchrls", q, k_band) * scale
    key_abs = (jnp.arange(nc)[:, None] - WINDOW_CHUNKS) * CHUNK + jnp.arange(BAND)[None, :]
    valid = key_abs >= PAD_LEAD
    s = jnp.where(valid[None, :, None, None, None, :], s, -jnp.inf)
    sink = sinks.astype(jnp.float32).reshape(ATT_KV_HEADS, ATT_REP)[None, None, :, :, None, None]
    m = jnp.maximum(jnp.max(s, axis=-1, keepdims=True), sink)
    pr = jnp.exp(s - m)
    denom = jnp.sum(pr, axis=-1, keepdims=True) + jnp.exp(sink - m)
    out = jnp.einsum("bchrls,bcshd->bclhrd", pr / denom, v_band)
    return out.reshape(bsz, lp, D_ATT)


def setup_inputs(seed: int = 0) -> dict:
    key = jax.random.key(seed)
    ks = jax.random.split(key, 14)
    f32 = jnp.float32
    x = jax.random.normal(ks[0], (BATCH, SEQ, D_MODEL), f32)
    meta_tokens = jax.random.normal(ks[1], (N_META, D_MODEL), f32)
    norm_pre_w = 1.0 + 0.01 * jax.random.normal(ks[2], (DEPTH, D_MODEL), f32)
    w_in = jax.random.normal(ks[3], (DEPTH, D_MODEL, D_IN_PROJ), f32) * D_MODEL ** -0.5
    conv_w = jax.random.normal(ks[4], (DEPTH, CONV_WIDTH, D_CONV), f32) * CONV_WIDTH ** -0.5
    conv_b = 0.02 * jax.random.normal(ks[5], (DEPTH, D_CONV), f32)
    dt0 = jnp.exp(jax.random.uniform(ks[6], (DEPTH, SSD_HEADS), f32,
                                     minval=math.log(1e-3), maxval=math.log(1e-1)))
    dt_bias = dt0 + jnp.log(-jnp.expm1(-dt0))
    a_log = jnp.log(jax.random.uniform(ks[7], (DEPTH, SSD_HEADS), f32, minval=1.0, maxval=16.0))
    d_skip = 1.0 + 0.01 * jax.random.normal(ks[8], (DEPTH, SSD_HEADS), f32)
    ssd_norm_w = 1.0 + 0.01 * jax.random.normal(ks[9], (DEPTH, D_SSD), f32)
    attn_sinks = 0.5 * jax.random.normal(ks[10], (DEPTH, ATT_Q_HEADS), f32)
    w_out = jax.random.normal(ks[11], (DEPTH, D_MIX, D_MODEL), f32) * D_MIX ** -0.5
    norm_post_w = 1.0 + 0.01 * jax.random.normal(ks[12], (DEPTH, D_MODEL), f32)
    return {"x": x, "meta_tokens": meta_tokens, "norm_pre_w": norm_pre_w, "w_in": w_in,
            "conv_w": conv_w, "conv_b": conv_b, "dt_bias": dt_bias, "a_log": a_log,
            "d_skip": d_skip, "ssd_norm_w": ssd_norm_w, "attn_sinks": attn_sinks,
            "w_out": w_out, "norm_post_w": norm_post_w}


def reference(x, meta_tokens, norm_pre_w, w_in, conv_w, conv_b, dt_bias, a_log, d_skip,
              ssd_norm_w, attn_sinks, w_out, norm_post_w):
    bsz, seq, _ = x.shape
    meta = jnp.broadcast_to(meta_tokens[None].astype(x.dtype), (bsz, N_META, D_MODEL))
    h = jnp.concatenate([meta, x], axis=1)
    lp = N_META + seq + PAD_LEAD
    idx = jnp.arange(lp)
    pos = idx - PAD_LEAD
    valid = (idx >= PAD_LEAD).astype(jnp.float32)
    split_pts = [D_SSD, D_SSD + D_CONV, D_SSD + D_CONV + SSD_HEADS,
                 D_SSD + D_CONV + SSD_HEADS + D_ATT,
                 D_SSD + D_CONV + SSD_HEADS + D_ATT + D_KV,
                 D_SSD + D_CONV + SSD_HEADS + D_ATT + 2 * D_KV]
    for layer in range(DEPTH):
        hn = rmsnorm(h, norm_pre_w[layer])
        proj = jnp.matmul(hn, w_in[layer]).astype(jnp.float32)
        proj = jnp.pad(proj, ((0, 0), (PAD_LEAD, 0), (0, 0)))
        z, xbc, dt_raw, q, k, v, g_att = jnp.split(proj, split_pts, axis=-1)

        xbc = jax.nn.silu(causal_depthwise_conv(xbc, conv_w[layer], conv_b[layer]))
        xs, b_mat, c_mat = jnp.split(xbc, [D_SSD, D_SSD + SSD_GROUPS * D_STATE], axis=-1)
        dt = jax.nn.softplus(dt_raw + dt_bias[layer].astype(jnp.float32)) * valid[None, :, None]
        a = -jnp.exp(a_log[layer].astype(jnp.float32))
        xs = xs.reshape(bsz, lp, SSD_GROUPS, SSD_HEADS_PER_GROUP, SSD_HEAD_DIM)
        y = ssd_chunked_scan(xs, dt.reshape(bsz, lp, SSD_GROUPS, SSD_HEADS_PER_GROUP),
                             a.reshape(SSD_GROUPS, SSD_HEADS_PER_GROUP),
                             b_mat.reshape(bsz, lp, SSD_GROUPS, D_STATE),
                             c_mat.reshape(bsz, lp, SSD_GROUPS, D_STATE))
        y = y + d_skip[layer].astype(jnp.float32).reshape(SSD_GROUPS, SSD_HEADS_PER_GROUP)[..., None] * xs
        y = y.reshape(bsz, lp, D_SSD) * jax.nn.silu(z)
        y = rmsnorm(y.reshape(bsz, lp, SSD_GROUPS, D_SSD // SSD_GROUPS),
                    jnp.ones((D_SSD // SSD_GROUPS,), jnp.float32)).reshape(bsz, lp, D_SSD)
        y = y * ssd_norm_w[layer].astype(jnp.float32)

        q = rope(q.reshape(bsz, lp, ATT_Q_HEADS, ATT_HEAD_DIM), pos)
        k = rope(k.reshape(bsz, lp, ATT_KV_HEADS, ATT_HEAD_DIM), pos)
        v = v.reshape(bsz, lp, ATT_KV_HEADS, ATT_HEAD_DIM)
        att = banded_sink_attention(q, k, v, attn_sinks[layer]) * jax.nn.silu(g_att)

        mix = jnp.concatenate([y, att], axis=-1)[:, PAD_LEAD:].astype(h.dtype)
        out = jnp.matmul(mix, w_out[layer])
        h = h + rmsnorm(out, norm_post_w[layer])
    return h[:, N_META:]
```

```python
import functools
import math

import jax
import jax.numpy as jnp
from jax import lax
from jax.experimental import pallas as pl
from jax.experimental.pallas import tpu as pltpu

D_MODEL = 2048
CHUNK = 64
N_META = 16
PAD_LEAD = CHUNK - N_META
EPS = 1e-6

SSD_HEADS = 32
SSD_HEAD_DIM = 64
D_SSD = SSD_HEADS * SSD_HEAD_DIM
SSD_GROUPS = 8
D_STATE = 128
CONV_WIDTH = 4
D_BC = 2 * SSD_GROUPS * D_STATE
D_CONV = D_SSD + D_BC
GROUP_W = D_SSD // SSD_GROUPS

ATT_Q_HEADS = 16
ATT_KV_HEADS = 4
ATT_REP = ATT_Q_HEADS // ATT_KV_HEADS
ATT_HEAD_DIM = 64
D_ATT = ATT_Q_HEADS * ATT_HEAD_DIM
D_KV = ATT_KV_HEADS * ATT_HEAD_DIM
WINDOW_CHUNKS = 2
BAND = (WINDOW_CHUNKS + 1) * CHUNK
ROPE_THETA = 10000.0
D_MIX = D_SSD + D_ATT

LANES = 128
DT_PAD = LANES
D_PROJ = D_SSD + D_SSD + D_BC + D_ATT + D_ATT + D_KV + D_KV
NEG_BIG = -1e30
VMEM_LIMIT = 56 * 1024 * 1024

F32 = jnp.float32
BF16 = jnp.bfloat16


def _dot(a, b):
    return jnp.dot(a, b, preferred_element_type=F32)


def _dot_nt(a, b):
    return lax.dot_general(a, b, (((1,), (1,)), ((), ())), preferred_element_type=F32)


def _dot_tn(a, b):
    return lax.dot_general(a, b, (((0,), (0,)), ((), ())), preferred_element_type=F32)


def _silu(x):
    return x * (1.0 / (1.0 + jnp.exp(-x)))


def _split_bf16(x, parts):
    out = []
    r = x
    for _ in range(parts):
        p = r.astype(BF16)
        out.append(p)
        r = r - p.astype(F32)
    return out


def _in_proj_kernel(x_ref, nw_ref, w_ref, wdt_ref, proj_ref, dt_ref, xn_ref, *, strip):
    j = pl.program_id(1)
    tm = x_ref.shape[0]

    @pl.when(j == 0)
    def _():
        def body(s, carry):
            r0 = pl.multiple_of(s * strip, strip)
            xv = x_ref[pl.ds(r0, strip), :]
            ms = jnp.mean(xv * xv, axis=-1, keepdims=True)
            xn = (xv * lax.rsqrt(ms + EPS) * nw_ref[...]).astype(BF16)
            xn_ref[pl.ds(r0, strip), :] = xn
            dt_ref[pl.ds(r0, strip), :] = _dot(xn, wdt_ref[...])
            return carry
        lax.fori_loop(0, tm // strip, body, 0)

    proj_ref[...] = _dot(xn_ref[...], w_ref[...]).astype(BF16)


def _in_proj(x2d, norm_w, w_main, w_dt, *, tm, tn):
    rows = x2d.shape[0]
    strip = min(tm, 128)
    grid = (rows // tm, D_PROJ // tn)
    return pl.pallas_call(
        functools.partial(_in_proj_kernel, strip=strip),
        grid=grid,
        in_specs=[
            pl.BlockSpec((tm, D_MODEL), lambda i, j: (i, 0)),
            pl.BlockSpec((1, D_MODEL), lambda i, j: (0, 0)),
            pl.BlockSpec((D_MODEL, tn), lambda i, j: (0, j)),
            pl.BlockSpec((D_MODEL, DT_PAD), lambda i, j: (0, 0)),
        ],
        out_specs=[
            pl.BlockSpec((tm, tn), lambda i, j: (i, j)),
            pl.BlockSpec((tm, DT_PAD), lambda i, j: (i, 0)),
        ],
        out_shape=[
            jax.ShapeDtypeStruct((rows, D_PROJ), BF16),
            jax.ShapeDtypeStruct((rows, DT_PAD), F32),
        ],
        scratch_shapes=[pltpu.VMEM((tm, D_MODEL), BF16)],
        compiler_params=pltpu.CompilerParams(
            dimension_semantics=("arbitrary", "arbitrary"), vmem_limit_bytes=VMEM_LIMIT),
        name="in_proj",
    )(x2d, norm_w, w_main, w_dt)


def _lane_iota(shape):
    return lax.broadcasted_iota(jnp.int32, shape, 1)


def _row_iota(shape):
    return lax.broadcasted_iota(jnp.int32, shape, 0)


def _conv_silu(xs_ref, bc_ref, row0, xbuf_ref, convw_ref, convb_ref, xact_ref, bact_ref, cact_ref):
    rows = pl.ds(row0, CHUNK)
    xbuf_ref[8:8 + CHUNK, 0:D_SSD] = xs_ref[rows, :].astype(F32)
    xbuf_ref[8:8 + CHUNK, D_SSD:D_CONV] = bc_ref[rows, :].astype(F32)
    strip = 512
    for c0 in range(0, D_CONV, strip):
        cols = slice(c0, c0 + strip)
        acc = convb_ref[:, cols] + convw_ref[0:1, cols] * xbuf_ref[5:5 + CHUNK, cols]
        for i in range(1, CONV_WIDTH):
            acc = acc + convw_ref[i:i + 1, cols] * xbuf_ref[5 + i:5 + i + CHUNK, cols]
        act = _silu(acc)
        if c0 < D_SSD:
            xact_ref[:, cols] = act
        elif c0 < D_SSD + D_BC // 2:
            bact_ref[:, c0 - D_SSD:c0 - D_SSD + strip] = act
        else:
            o = c0 - D_SSD - D_BC // 2
            cact_ref[:, o:o + strip] = act.astype(BF16)
    xbuf_ref[0:8, :] = xbuf_ref[CHUNK:CHUNK + 8, :]


def _decay_terms(dt_ref, row0, dtb_ref, alog_ref, e_ref, valid):
    dtr = dt_ref[pl.ds(row0, CHUNK), :]
    xx = dtr + dtb_ref[...]
    dt = jnp.maximum(xx, 0.0) + jnp.log1p(jnp.exp(-jnp.abs(xx)))
    if valid is not None:
        dt = dt * valid
    head_lane = _lane_iota((CHUNK, LANES)) < SSD_HEADS
    dt = jnp.where(head_lane, dt, 0.0)
    dta = dt * (-jnp.exp(alog_ref[...]))
    rr = _row_iota((CHUNK, 3 * CHUNK))
    cc = _lane_iota((CHUNK, 3 * CHUNK)) & (CHUNK - 1)
    tril3 = jnp.where(rr >= cc, 1.0, 0.0).astype(BF16)
    acs = _dot(tril3, jnp.concatenate(_split_bf16(dta, 3), axis=0))
    a_hi, a_mid, a_lo = [p.astype(F32) for p in _split_bf16(acs, 3)]
    a_stack = (a_hi + pltpu.roll(a_mid, 32, 1) + pltpu.roll(a_lo, 64, 1)).astype(BF16)
    acs_x = _dot(a_stack, e_ref[...])
    d_hi, d_mid = [p.astype(F32) for p in _split_bf16(dt, 2)]
    d_stack = (d_hi + pltpu.roll(d_mid, 32, 1)).astype(BF16)
    dt_x = _dot(d_stack, e_ref[...])
    return dt_x, acs_x


def _ssd_chunk(z_ref, row0, xact_ref, bact_ref, cact_ref, dt_x, acs_x, state_ref,
               dskip_ref, nw_ref, out_ref, *, emit):
    lane = _lane_iota((CHUNK, LANES))
    row = _row_iota((CHUNK, LANES))
    s_idx = lane & (CHUNK - 1)
    diag = row == s_idx
    causal = row >= s_idx
    low_half = lane < CHUNK
    rows = pl.ds(row0, CHUNK)

    for g in range(SSD_GROUPS):
        gl = slice(g * GROUP_W, (g + 1) * GROUP_W)
        nl = slice(g * D_STATE, (g + 1) * D_STATE)
        acs_g = acs_x[:, gl]
        a_last = acs_g[CHUNK - 1:CHUNK, :]
        x_g = xact_ref[:, gl]
        x_dt = x_g * dt_x[:, gl]
        b_f32 = bact_ref[:, nl]
        b_g = b_f32.astype(BF16)
        xw = (x_dt * jnp.exp(a_last - acs_g)).astype(BF16)
        s_prev = state_ref[g]
        if emit:
            c_g = cact_ref[:, nl]
            cb2 = _dot_nt(c_g, jnp.concatenate([b_g, b_g], axis=0))
            y_parts = []
            for pp in range(2):
                pls = slice(pp * LANES, (pp + 1) * LANES)
                col = acs_g[:, pls]
                rowv = jnp.sum(jnp.where(diag, col, 0.0), axis=0, keepdims=True)
                seg = jnp.where(causal, col - rowv, NEG_BIG)
                lmat = (jnp.exp(seg) * cb2).astype(BF16)
                xp = x_dt[:, pls]
                xbd = jnp.concatenate([jnp.where(low_half, xp, 0.0),
                                       jnp.where(low_half, 0.0, xp)], axis=0).astype(BF16)
                y_parts.append(_dot(lmat, xbd))
            y_diag = jnp.concatenate(y_parts, axis=1)
            y_off = _dot(c_g, s_prev.astype(BF16)) * jnp.exp(acs_g)
            y = y_diag + y_off + dskip_ref[:, gl] * x_g
            zf = z_ref[rows, gl].astype(F32)
            yz = y * _silu(zf)
            ms = jnp.mean(yz * yz, axis=-1, keepdims=True)
            out_ref[rows, gl] = (yz * lax.rsqrt(ms + EPS) * nw_ref[:, gl]).astype(BF16)
        state_ref[g] = s_prev * jnp.exp(a_last) + _dot_tn(b_g, xw)


def _rope_kv(k_ref, v_ref, row0, ck_ref, sk_ref, kk_ref, vv_ref, slot):
    rows = pl.ds(row0, CHUNK)
    lane = _lane_iota((CHUNK, LANES))
    first_of_pair = (lane & 63) < 32
    low_half = lane < CHUNK
    ck = ck_ref[rows, :]
    sk = sk_ref[rows, :]
    dst = slice(slot * CHUNK, (slot + 1) * CHUNK)
    for c in range(2):
        kx = k_ref[rows, c * LANES:(c + 1) * LANES].astype(F32)
        kr = kx * ck + pltpu.roll(kx, 64, 1) * sk
        kk_ref[2 * c, dst, :] = jnp.where(first_of_pair, kr, pltpu.roll(kr, 32, 1)).astype(BF16)
        kk_ref[2 * c + 1, dst, :] = jnp.where(first_of_pair, pltpu.roll(kr, 96, 1), kr).astype(BF16)
        vx = v_ref[rows, c * LANES:(c + 1) * LANES].astype(F32)
        vr = pltpu.roll(vx, 64, 1)
        vv_ref[2 * c, dst, 0:LANES] = jnp.where(low_half, vx, vr).astype(BF16)
        vv_ref[2 * c + 1, dst, 0:LANES] = jnp.where(low_half, vr, vx).astype(BF16)


def _attention_chunk(q_ref, g_ref, row0, cq_ref, sq_ref, kk_ref, vv_ref, sinks_ref, bias, out_ref):
    rows = pl.ds(row0, CHUNK)
    lane = _lane_iota((CHUNK, LANES))
    first_of_pair = (lane & 63) < 32
    low_half = lane < CHUNK
    cq = cq_ref[rows, :]
    sq = sq_ref[rows, :]
    for kv in range(ATT_KV_HEADS):
        q_rows = []
        for c in (2 * kv, 2 * kv + 1):
            qx = q_ref[rows, c * LANES:(c + 1) * LANES].astype(F32)
            qr = qx * cq + pltpu.roll(qx, 64, 1) * sq
            q_rows.append(jnp.where(first_of_pair, qr, 0.0).astype(BF16))
            q_rows.append(jnp.where(first_of_pair, 0.0, qr).astype(BF16))
        q_stack = jnp.concatenate(q_rows, axis=0)
        s = _dot_nt(q_stack, kk_ref[kv]) + bias
        p_rows, e_sink = [], []
        for r in range(ATT_REP):
            s_r = s[r * CHUNK:(r + 1) * CHUNK, :]
            sink = sinks_ref[kv * ATT_REP + r]
            m = jnp.maximum(jnp.max(s_r, axis=-1, keepdims=True), sink)
            p_rows.append(jnp.exp(s_r - m).astype(BF16))
            e_sink.append(jnp.exp(sink - m))
        pv = _dot(jnp.concatenate(p_rows, axis=0), vv_ref[kv])
        for pr in range(2):
            r0, r1 = 2 * pr, 2 * pr + 1
            blk0 = pv[r0 * CHUNK:(r0 + 1) * CHUNK, :]
            blk1 = pv[r1 * CHUNK:(r1 + 1) * CHUNK, :]
            num = jnp.where(low_half, blk0[:, 0:LANES], blk1[:, 0:LANES])
            den = jnp.where(low_half, blk0[:, LANES:] + e_sink[r0], blk1[:, LANES:] + e_sink[r1])
            col = (2 * kv + pr) * LANES
            gate = _silu(g_ref[rows, col:col + LANES].astype(F32))
            out_ref[rows, D_SSD + col:D_SSD + col + LANES] = (num / den * gate).astype(BF16)


def _meta_kernel(xs_ref, bc_ref, k_ref, v_ref, dt_ref, ck_ref, sk_ref,
                 convw_ref, convb_ref, dtb_ref, alog_ref, e_ref,
                 state_ref, tail_ref, kk_ref, vv_ref,
                 xbuf_ref, xact_ref, bact_ref, cact_ref):
    xbuf_ref[0:8, :] = jnp.zeros((8, D_CONV), F32)
    state_ref[...] = jnp.zeros(state_ref.shape, F32)
    _conv_silu(xs_ref, bc_ref, 0, xbuf_ref, convw_ref, convb_ref, xact_ref, bact_ref, cact_ref)
    tail_ref[...] = xbuf_ref[0:8, :]
    valid = jnp.where(_row_iota((CHUNK, LANES)) >= PAD_LEAD, 1.0, 0.0)
    dt_x, acs_x = _decay_terms(dt_ref, 0, dtb_ref, alog_ref, e_ref, valid)
    _ssd_chunk(None, 0, xact_ref, bact_ref, cact_ref, dt_x, acs_x, state_ref,
               None, None, None, emit=False)
    _rope_kv(k_ref, v_ref, 0, ck_ref, sk_ref, kk_ref, vv_ref, 0)


def _mixer_kernel(z_ref, xs_ref, bc_ref, q_ref, g_ref, k_ref, v_ref, dt_ref,
                  cq_ref, sq_ref, ck_ref, sk_ref,
                  state0_ref, tail0_ref, kk0_ref, vv0_ref,
                  convw_ref, convb_ref, dtb_ref, alog_ref, dskip_ref, nw_ref, e_ref, sinks_ref,
                  out_ref,
                  state_ref, xbuf_ref, xact_ref, bact_ref, cact_ref, kk_ref, vv_ref, *, chunks):
    j = pl.program_id(1)

    @pl.when(j == 0)
    def _():
        state_ref[...] = state0_ref[...]
        xbuf_ref[0:8, :] = tail0_ref[...]
        kk_ref[...] = jnp.zeros(kk_ref.shape, BF16)
        vv_ref[:, :, 0:LANES] = jnp.zeros((ATT_KV_HEADS, BAND, LANES), BF16)
        vv_ref[:, :, LANES:] = jnp.ones((ATT_KV_HEADS, BAND, LANES), BF16)
        kk_ref[:, 2 * CHUNK:BAND, :] = kk0_ref[...]
        vv_ref[:, 2 * CHUNK:BAND, 0:LANES] = vv0_ref[...]

    def body(ci, carry):
        row0 = pl.multiple_of(ci * CHUNK, CHUNK)
        _conv_silu(xs_ref, bc_ref, row0, xbuf_ref, convw_ref, convb_ref, xact_ref, bact_ref, cact_ref)
        dt_x, acs_x = _decay_terms(dt_ref, row0, dtb_ref, alog_ref, e_ref, None)
        _ssd_chunk(z_ref, row0, xact_ref, bact_ref, cact_ref, dt_x, acs_x, state_ref,
                   dskip_ref, nw_ref, out_ref, emit=True)
        kk_ref[:, 0:2 * CHUNK, :] = kk_ref[:, CHUNK:BAND, :]
        vv_ref[:, 0:2 * CHUNK, 0:LANES] = vv_ref[:, CHUNK:BAND, 0:LANES]
        _rope_kv(k_ref, v_ref, row0, ck_ref, sk_ref, kk_ref, vv_ref, WINDOW_CHUNKS)
        chunk_idx = j * chunks + ci + 1
        key_abs = (chunk_idx - WINDOW_CHUNKS) * CHUNK + _lane_iota((1, BAND))
        bias = jnp.where(key_abs >= PAD_LEAD, 0.0, NEG_BIG)
        _attention_chunk(q_ref, g_ref, row0, cq_ref, sq_ref, kk_ref, vv_ref, sinks_ref, bias, out_ref)
        return carry

    lax.fori_loop(0, chunks, body, 0)


def _const_spec(shape):
    nd = len(shape)
    return pl.BlockSpec(shape, lambda *_: (0,) * nd)


def _meta_state(projm, dtm, ckm, skm, conv_w, conv_b, dtb, alog, e_mat):
    col = lambda w, idx: pl.BlockSpec((CHUNK, w), lambda i: (0, idx))
    return pl.pallas_call(
        _meta_kernel,
        grid=(1,),
        in_specs=[
            col(D_SSD, 1), col(D_BC, 2), col(D_KV, 32), col(D_KV, 33),
            _const_spec((CHUNK, DT_PAD)), _const_spec((CHUNK, LANES)), _const_spec((CHUNK, LANES)),
            _const_spec((CONV_WIDTH, D_CONV)), _const_spec((1, D_CONV)),
            _const_spec((1, DT_PAD)), _const_spec((1, DT_PAD)), _const_spec((LANES, D_SSD)),
        ],
        out_specs=[
            _const_spec((SSD_GROUPS, D_STATE, GROUP_W)), _const_spec((8, D_CONV)),
            _const_spec((ATT_KV_HEADS, CHUNK, LANES)), _const_spec((ATT_KV_HEADS, CHUNK, LANES)),
        ],
        out_shape=[
            jax.ShapeDtypeStruct((SSD_GROUPS, D_STATE, GROUP_W), F32),
            jax.ShapeDtypeStruct((8, D_CONV), F32),
            jax.ShapeDtypeStruct((ATT_KV_HEADS, CHUNK, LANES), BF16),
            jax.ShapeDtypeStruct((ATT_KV_HEADS, CHUNK, LANES), BF16),
        ],
        scratch_shapes=[
            pltpu.VMEM((CHUNK + 8, D_CONV), F32),
            pltpu.VMEM((CHUNK, D_SSD), F32),
            pltpu.VMEM((CHUNK, D_BC // 2), F32),
            pltpu.VMEM((CHUNK, D_BC // 2), BF16),
        ],
        compiler_params=pltpu.CompilerParams(
            dimension_semantics=("arbitrary",), vmem_limit_bytes=VMEM_LIMIT),
        name="meta_state",
    )(projm, projm, projm, projm, dtm, ckm, skm, conv_w, conv_b, dtb, alog, e_mat)


def _mixer(proj, dtr, tables, meta_state, params, *, batch, seq, tb):
    cq, sq, ck, sk = tables
    state0, tail0, kk0, vv0 = meta_state
    conv_w, conv_b, dtb, alog, dskip_x, ssd_nw, e_mat, sinks = params
    nblk = seq // tb
    rows = batch * seq
    col = lambda w, idx: pl.BlockSpec((tb, w), lambda b, j: (b * nblk + j, idx))
    tab = pl.BlockSpec((tb, LANES), lambda b, j: (j, 0))
    return pl.pallas_call(
        functools.partial(_mixer_kernel, chunks=tb // CHUNK),
        grid=(batch, nblk),
        in_specs=[
            col(D_SSD, 0), col(D_SSD, 1), col(D_BC, 2), col(D_ATT, 6), col(D_ATT, 7),
            col(D_KV, 32), col(D_KV, 33), col(DT_PAD, 0),
            tab, tab, tab, tab,
            _const_spec((SSD_GROUPS, D_STATE, GROUP_W)), _const_spec((8, D_CONV)),
            _const_spec((ATT_KV_HEADS, CHUNK, LANES)), _const_spec((ATT_KV_HEADS, CHUNK, LANES)),
            _const_spec((CONV_WIDTH, D_CONV)), _const_spec((1, D_CONV)),
            _const_spec((1, DT_PAD)), _const_spec((1, DT_PAD)),
            _const_spec((1, D_SSD)), _const_spec((1, D_SSD)), _const_spec((LANES, D_SSD)),
            pl.BlockSpec(memory_space=pltpu.SMEM),
        ],
        out_specs=pl.BlockSpec((tb, D_MIX), lambda b, j: (b * nblk + j, 0)),
        out_shape=jax.ShapeDtypeStruct((rows, D_MIX), BF16),
        scratch_shapes=[
            pltpu.VMEM((SSD_GROUPS, D_STATE, GROUP_W), F32),
            pltpu.VMEM((CHUNK + 8, D_CONV), F32),
            pltpu.VMEM((CHUNK, D_SSD), F32),
            pltpu.VMEM((CHUNK, D_BC // 2), F32),
            pltpu.VMEM((CHUNK, D_BC // 2), BF16),
            pltpu.VMEM((ATT_KV_HEADS, BAND, LANES), BF16),
            pltpu.VMEM((ATT_KV_HEADS, BAND, 2 * LANES), BF16),
        ],
        compiler_params=pltpu.CompilerParams(
            dimension_semantics=("arbitrary", "arbitrary"), vmem_limit_bytes=VMEM_LIMIT),
        name="mixer",
    )(proj, proj, proj, proj, proj, proj, proj, dtr, cq, sq, ck, sk,
      state0, tail0, kk0, vv0, conv_w, conv_b, dtb, alog, dskip_x, ssd_nw, e_mat, sinks)


def _out_proj_kernel(mix_ref, w_ref, x_ref, nw_ref, o_ref):
    o = _dot(mix_ref[...], w_ref[...])
    ms = jnp.mean(o * o, axis=-1, keepdims=True)
    o_ref[...] = x_ref[...] + o * lax.rsqrt(ms + EPS) * nw_ref[...]


def _out_proj(mix, w_out, x2d, norm_w, *, tm):
    rows = mix.shape[0]
    return pl.pallas_call(
        _out_proj_kernel,
        grid=(rows // tm,),
        in_specs=[
            pl.BlockSpec((tm, D_MIX), lambda i: (i, 0)),
            pl.BlockSpec((D_MIX, D_MODEL), lambda i: (0, 0)),
            pl.BlockSpec((tm, D_MODEL), lambda i: (i, 0)),
            pl.BlockSpec((1, D_MODEL), lambda i: (0, 0)),
        ],
        out_specs=pl.BlockSpec((tm, D_MODEL), lambda i: (i, 0)),
        out_shape=jax.ShapeDtypeStruct((rows, D_MODEL), F32),
        compiler_params=pltpu.CompilerParams(
            dimension_semantics=("arbitrary",), vmem_limit_bytes=VMEM_LIMIT),
        name="out_proj",
    )(mix, w_out, x2d, norm_w)


def _pair_layout(w, heads):
    half = ATT_HEAD_DIM // 2
    w = w.reshape(w.shape[0], heads // 2, 2, 2, half)
    return w.transpose(0, 1, 3, 2, 4).reshape(w.shape[0], heads * ATT_HEAD_DIM)


def _rope_tables(n_pos):
    half = ATT_HEAD_DIM // 2
    pos = jnp.arange(n_pos, dtype=jnp.int32) - PAD_LEAD
    inv = ROPE_THETA ** (-jnp.arange(half, dtype=F32) / half)
    ang = pos.astype(F32)[:, None] * inv[None, :]
    cos, sin = jnp.cos(ang), jnp.sin(ang)
    cos4 = jnp.concatenate([cos, cos, cos, cos], axis=1)
    sin4 = jnp.concatenate([-sin, -sin, sin, sin], axis=1)
    return cos4, sin4


def kernel(x, meta_tokens, norm_pre_w, w_in, conv_w, conv_b, dt_bias, a_log, d_skip, ssd_norm_w,
           attn_sinks, w_out, norm_post_w):
    batch, seq, _ = x.shape
    assert norm_pre_w.shape[0] == 1 and seq % CHUNK == 0
    rows = batch * seq
    x2d = x.reshape(rows, D_MODEL)

    w = w_in[0]
    o = 0
    seg = {}
    for name, width in (("z", D_SSD), ("xs", D_SSD), ("bc", D_BC), ("dt", SSD_HEADS),
                        ("q", D_ATT), ("k", D_KV), ("v", D_KV), ("g", D_ATT)):
        seg[name] = w[:, o:o + width]
        o += width
    w_main = jnp.concatenate(
        [seg["z"], seg["xs"], seg["bc"], _pair_layout(seg["q"], ATT_Q_HEADS), seg["g"],
         _pair_layout(seg["k"], ATT_KV_HEADS), seg["v"]], axis=1).astype(BF16)
    w_dt = jnp.pad(seg["dt"], ((0, 0), (0, DT_PAD - SSD_HEADS))).astype(BF16)
    w_o = w_out[0].astype(BF16)

    pad_h = lambda v: jnp.pad(v.reshape(1, SSD_HEADS), ((0, 0), (0, DT_PAD - SSD_HEADS)))
    dtb = pad_h(dt_bias[0])
    alog = pad_h(a_log[0])
    dskip_x = jnp.repeat(d_skip[0], SSD_HEAD_DIM).reshape(1, D_SSD)
    e_rows = jnp.arange(LANES)
    e_mat = ((e_rows[:, None] % SSD_HEADS == jnp.arange(D_SSD)[None, :] // SSD_HEAD_DIM)
             & (e_rows[:, None] < 3 * SSD_HEADS)).astype(BF16)

    cos4, sin4 = _rope_tables(CHUNK + seq)
    scale = ATT_HEAD_DIM ** -0.5
    tables = (cos4[CHUNK:] * scale, sin4[CHUNK:] * scale, cos4[CHUNK:], sin4[CHUNK:])

    npw = norm_pre_w[0].reshape(1, D_MODEL)
    cw, cb = conv_w[0], conv_b[0].reshape(1, D_CONV)

    xm = jnp.concatenate([jnp.zeros((PAD_LEAD, D_MODEL), x.dtype), meta_tokens.astype(x.dtype)], axis=0)
    projm, dtm = _in_proj(xm, npw, w_main, w_dt, tm=CHUNK, tn=2176)
    meta_state = _meta_state(projm, dtm, cos4[:CHUNK], sin4[:CHUNK], cw, cb, dtb, alog, e_mat)

    proj, dtr = _in_proj(x2d, npw, w_main, w_dt, tm=512, tn=2176)
    params = (cw, cb, dtb, alog, dskip_x, ssd_norm_w[0].reshape(1, D_SSD), e_mat, attn_sinks[0])
    mix = _mixer(proj, dtr, tables, meta_state, params, batch=batch, seq=seq, tb=512)
    out = _out_proj(mix, w_o, x2d, norm_post_w[0].reshape(1, D_MODEL), tm=256)
    return out.reshape(batch, seq, D_MODEL)
```

```python
import functools

import jax
import jax.numpy as jnp
from jax import lax
from jax.experimental import pallas as pl
from jax.experimental.pallas import tpu as pltpu

D_MODEL = 2048
CHUNK = 64
N_META = 16
PAD_LEAD = CHUNK - N_META
EPS = 1e-6

SSD_HEADS = 32
SSD_HEAD_DIM = 64
D_SSD = SSD_HEADS * SSD_HEAD_DIM
SSD_GROUPS = 8
D_STATE = 128
CONV_WIDTH = 4
D_BC = 2 * SSD_GROUPS * D_STATE
D_CONV = D_SSD + D_BC
GROUP_W = D_SSD // SSD_GROUPS

ATT_Q_HEADS = 16
ATT_KV_HEADS = 4
ATT_REP = ATT_Q_HEADS // ATT_KV_HEADS
ATT_HEAD_DIM = 64
D_ATT = ATT_Q_HEADS * ATT_HEAD_DIM
D_KV = ATT_KV_HEADS * ATT_HEAD_DIM
WINDOW_CHUNKS = 2
BAND = (WINDOW_CHUNKS + 1) * CHUNK
ROPE_THETA = 10000.0
D_MIX = D_SSD + D_ATT

LANES = 128
DT_PAD = LANES
D_PROJ = D_SSD + D_SSD + D_BC + D_ATT + D_ATT + D_KV + D_KV
CONV_TILES = D_CONV // LANES
NEG_BIG = -1e30
LOG2E = 1.4426950408889634
VMEM_LIMIT = 56 * 1024 * 1024

F32 = jnp.float32
BF16 = jnp.bfloat16


def _dot(a, b):
    return jnp.dot(a, b, preferred_element_type=F32)


def _dot_nt(a, b):
    return lax.dot_general(a, b, (((1,), (1,)), ((), ())), preferred_element_type=F32)


def _dot_tn(a, b):
    return lax.dot_general(a, b, (((0,), (0,)), ((), ())), preferred_element_type=F32)


def _silu(x):
    return x * (1.0 / (1.0 + jnp.exp(-x)))


def _split_bf16(x, parts):
    out = []
    r = x
    for _ in range(parts):
        p = r.astype(BF16)
        out.append(p)
        r = r - p.astype(F32)
    return out


def _lane_iota(shape):
    return lax.broadcasted_iota(jnp.int32, shape, 1)


def _row_iota(shape):
    return lax.broadcasted_iota(jnp.int32, shape, 0)


def _in_proj_kernel(x_ref, nw_ref, w_ref, wdt_ref, proj_ref, dt_ref, xn_ref, *, strip):
    j = pl.program_id(1)
    tm = x_ref.shape[0]

    @pl.when(j == 0)
    def _():
        def body(s, carry):
            r0 = pl.multiple_of(s * strip, strip)
            xv = x_ref[pl.ds(r0, strip), :]
            ms = jnp.mean(xv * xv, axis=-1, keepdims=True)
            xn = (xv * lax.rsqrt(ms + EPS) * nw_ref[...]).astype(BF16)
            xn_ref[pl.ds(r0, strip), :] = xn
            dt_ref[pl.ds(r0, strip), :] = _dot(xn, wdt_ref[...])
            return carry
        lax.fori_loop(0, tm // strip, body, 0)

    proj_ref[...] = _dot(xn_ref[...], w_ref[...]).astype(BF16)


def _in_proj(x2d, norm_w, w_main, w_dt, *, tm, tn):
    rows = x2d.shape[0]
    strip = min(tm, 128)
    grid = (rows // tm, D_PROJ // tn)
    return pl.pallas_call(
        functools.partial(_in_proj_kernel, strip=strip),
        grid=grid,
        in_specs=[
            pl.BlockSpec((tm, D_MODEL), lambda i, j: (i, 0)),
            pl.BlockSpec((1, D_MODEL), lambda i, j: (0, 0)),
            pl.BlockSpec((D_MODEL, tn), lambda i, j: (0, j)),
            pl.BlockSpec((D_MODEL, DT_PAD), lambda i, j: (0, 0)),
        ],
        out_specs=[
            pl.BlockSpec((tm, tn), lambda i, j: (i, j)),
            pl.BlockSpec((tm, DT_PAD), lambda i, j: (i, 0)),
        ],
        out_shape=[
            jax.ShapeDtypeStruct((rows, D_PROJ), BF16),
            jax.ShapeDtypeStruct((rows, DT_PAD), F32),
        ],
        scratch_shapes=[pltpu.VMEM((tm, D_MODEL), BF16)],
        compiler_params=pltpu.CompilerParams(
            dimension_semantics=("arbitrary", "arbitrary"), vmem_limit_bytes=VMEM_LIMIT),
        name="in_proj",
    )(x2d, norm_w, w_main, w_dt)


def _conv_tile(t, xs_ref, bc_ref, row0, xbuf_ref, convw_ref, convb_ref, xact_ref, bact_ref, cact_ref):
    rows = pl.ds(row0, CHUNK)
    cols = slice(t * LANES, (t + 1) * LANES)
    if t < D_SSD // LANES:
        raw = xs_ref[rows, cols]
    else:
        raw = bc_ref[rows, t * LANES - D_SSD:(t + 1) * LANES - D_SSD]
    xbuf_ref[t, 8:8 + CHUNK, :] = raw.astype(F32)
    acc = convb_ref[:, cols] + convw_ref[0:1, cols] * xbuf_ref[t, 5:5 + CHUNK, :]
    for i in range(1, CONV_WIDTH):
        acc = acc + convw_ref[i:i + 1, cols] * xbuf_ref[t, 5 + i:5 + i + CHUNK, :]
    act = _silu(acc)
    if t < D_SSD // LANES:
        xact_ref[:, cols] = act
    elif t < (D_SSD + D_BC // 2) // LANES:
        bact_ref[:, t * LANES - D_SSD:(t + 1) * LANES - D_SSD] = act.astype(BF16)
    else:
        o = t * LANES - D_SSD - D_BC // 2
        cact_ref[:, o:o + LANES] = act.astype(BF16)
    xbuf_ref[t, 0:8, :] = xbuf_ref[t, CHUNK:CHUNK + 8, :]


def _dt_cumsum(dt_ref, row0, dtb_ref, alog_ref, valid):
    dtr = dt_ref[pl.ds(row0, CHUNK), :]
    xx = dtr + dtb_ref[...]
    dt = jnp.maximum(xx, 0.0) + jnp.log1p(jnp.exp(-jnp.abs(xx)))
    if valid is not None:
        dt = dt * valid
    dt = jnp.where(_lane_iota((CHUNK, LANES)) < SSD_HEADS, dt, 0.0)
    dta = dt * (-jnp.exp(alog_ref[...]) * LOG2E)
    rr = _row_iota((CHUNK, 3 * CHUNK))
    cc = _lane_iota((CHUNK, 3 * CHUNK)) & (CHUNK - 1)
    tril3 = jnp.where(rr >= cc, 1.0, 0.0).astype(BF16)
    acs = _dot(tril3, jnp.concatenate(_split_bf16(dta, 3), axis=0))
    return dt, acs


def _expand_heads(dt, acs, e_ref):
    a_hi, a_mid, a_lo = [p.astype(F32) for p in _split_bf16(acs, 3)]
    a_stack = (a_hi + pltpu.roll(a_mid, 32, 1) + pltpu.roll(a_lo, 64, 1)).astype(BF16)
    acs_x = _dot(a_stack, e_ref[...])
    d_hi, d_mid = [p.astype(F32) for p in _split_bf16(dt, 2)]
    d_stack = (d_hi + pltpu.roll(d_mid, 32, 1)).astype(BF16)
    dt_x = _dot(d_stack, e_ref[...])
    return dt_x, acs_x


def _state_update(g, xact_ref, bact_ref, dt_x, acs_x, state_ref):
    gl = slice(g * GROUP_W, (g + 1) * GROUP_W)
    acs_g = acs_x[:, gl]
    a_last = acs_g[CHUNK - 1:CHUNK, :]
    x_g = xact_ref[:, gl]
    x_dt = x_g * dt_x[:, gl]
    xw = (x_dt * jnp.exp2(a_last - acs_g)).astype(BF16)
    b_g = bact_ref[:, g * D_STATE:(g + 1) * D_STATE]
    state_ref[g] = state_ref[g] * jnp.exp2(a_last) + _dot_tn(b_g, xw)
    return x_g, x_dt


def _rope_kv(k_ref, v_ref, row0, ck_ref, sk_ref, kk_ref, vv_ref, slot):
    rows = pl.ds(row0, CHUNK)
    lane = _lane_iota((CHUNK, LANES))
    first_of_pair = (lane & 63) < 32
    low_half = lane < CHUNK
    ck = ck_ref[rows, :]
    sk = sk_ref[rows, :]
    dst = slice(slot * CHUNK, (slot + 1) * CHUNK)
    for c in range(2):
        kx = k_ref[rows, c * LANES:(c + 1) * LANES].astype(F32)
        kr = kx * ck + pltpu.roll(kx, 64, 1) * sk
        kk_ref[2 * c, dst, :] = jnp.where(first_of_pair, kr, pltpu.roll(kr, 32, 1)).astype(BF16)
        kk_ref[2 * c + 1, dst, :] = jnp.where(first_of_pair, pltpu.roll(kr, 96, 1), kr).astype(BF16)
        vx = v_ref[rows, c * LANES:(c + 1) * LANES].astype(F32)
        vr = pltpu.roll(vx, 64, 1)
        vv_ref[2 * c, dst, 0:LANES] = jnp.where(low_half, vx, vr).astype(BF16)
        vv_ref[2 * c + 1, dst, 0:LANES] = jnp.where(low_half, vr, vx).astype(BF16)


def _chunk(refs, row0, chunk_idx, act, conv_next):
    (z_ref, q_ref, g_ref, k_ref, v_ref, dt_ref, cq_ref, sq_ref, ck_ref, sk_ref,
     dtb_ref, alog_ref, dskip_ref, nw_ref, e_ref, sinks_ref, out_ref,
     state_ref, xbd_ref, kk_ref, vv_ref) = refs
    xact_ref, bact_ref, cact_ref = act
    rows = pl.ds(row0, CHUNK)
    lane = _lane_iota((CHUNK, LANES))
    row = _row_iota((CHUNK, LANES))
    first_of_pair = (lane & 63) < 32
    low_half = lane < CHUNK
    s_idx = lane & (CHUNK - 1)
    diag = row == s_idx
    causal = row >= s_idx

    dt, acs = _dt_cumsum(dt_ref, row0, dtb_ref, alog_ref, None)

    kk_ref[:, 0:2 * CHUNK, :] = kk_ref[:, CHUNK:BAND, :]
    vv_ref[:, 0:2 * CHUNK, 0:LANES] = vv_ref[:, CHUNK:BAND, 0:LANES]
    _rope_kv(k_ref, v_ref, row0, ck_ref, sk_ref, kk_ref, vv_ref, WINDOW_CHUNKS)
    key_abs = (chunk_idx - WINDOW_CHUNKS) * CHUNK + _lane_iota((1, BAND))
    bias = jnp.where(key_abs >= PAD_LEAD, 0.0, NEG_BIG)
    cq = cq_ref[rows, :]
    sq = sq_ref[rows, :]
    scores = []
    for kv in range(ATT_KV_HEADS):
        q_rows = []
        for c in (2 * kv, 2 * kv + 1):
            qx = q_ref[rows, c * LANES:(c + 1) * LANES].astype(F32)
            qr = qx * cq + pltpu.roll(qx, 64, 1) * sq
            q_rows.append(jnp.where(first_of_pair, qr, 0.0).astype(BF16))
            q_rows.append(jnp.where(first_of_pair, 0.0, qr).astype(BF16))
        q_stack = jnp.concatenate(q_rows, axis=0)
        scores.append(_dot_nt(q_stack, kk_ref[kv]) + bias)

    cb2, y_off_raw = [], []
    for g in range(SSD_GROUPS):
        nl = slice(g * D_STATE, (g + 1) * D_STATE)
        b_g = bact_ref[:, nl]
        c_g = cact_ref[:, nl]
        cb2.append(_dot_nt(c_g, jnp.concatenate([b_g, b_g], axis=0)))
        y_off_raw.append(_dot(c_g, state_ref[g].astype(BF16)))
    conv_next(0)

    dt_x, acs_x = _expand_heads(dt, acs, e_ref)

    pvs, e_sinks = [], []
    for kv in range(ATT_KV_HEADS):
        s = scores[kv]
        p_rows, e_sink = [], []
        for r in range(ATT_REP):
            s_r = s[r * CHUNK:(r + 1) * CHUNK, :]
            sink = sinks_ref[kv * ATT_REP + r] * LOG2E
            m = jnp.maximum(jnp.max(s_r, axis=-1, keepdims=True), sink)
            p_rows.append(jnp.exp2(s_r - m).astype(BF16))
            e_sink.append(jnp.exp2(sink - m))
        pvs.append(_dot(jnp.concatenate(p_rows, axis=0), vv_ref[kv]))
        e_sinks.append(e_sink)
    conv_next(1)

    xs_g, y_diag = [], []
    for g in range(SSD_GROUPS):
        x_g, x_dt = _state_update(g, xact_ref, bact_ref, dt_x, acs_x, state_ref)
        xs_g.append(x_g)
        parts = []
        for pp in range(2):
            pls = slice(g * GROUP_W + pp * LANES, g * GROUP_W + (pp + 1) * LANES)
            col = acs_x[:, pls]
            rowv = jnp.sum(jnp.where(diag, col, 0.0), axis=0, keepdims=True)
            seg = jnp.where(causal, col - rowv, NEG_BIG)
            lmat = (jnp.exp2(seg) * cb2[g]).astype(BF16)
            xp = x_dt[:, pp * LANES:(pp + 1) * LANES].astype(BF16)
            p = 2 * g + pp
            xbd_ref[p, 0:CHUNK, 0:CHUNK] = xp[:, 0:CHUNK]
            xbd_ref[p, CHUNK:, CHUNK:] = xp[:, CHUNK:]
            parts.append(_dot(lmat, xbd_ref[p]))
        y_diag.append(parts)
    conv_next(2)

    for kv in range(ATT_KV_HEADS):
        pv = pvs[kv]
        for pr in range(2):
            r0, r1 = 2 * pr, 2 * pr + 1
            blk0 = pv[r0 * CHUNK:(r0 + 1) * CHUNK, :]
            blk1 = pv[r1 * CHUNK:(r1 + 1) * CHUNK, :]
            num = jnp.where(low_half, blk0[:, 0:LANES], blk1[:, 0:LANES])
            den = jnp.where(low_half, blk0[:, LANES:] + e_sinks[kv][r0], blk1[:, LANES:] + e_sinks[kv][r1])
            col = (2 * kv + pr) * LANES
            gate = _silu(g_ref[rows, col:col + LANES].astype(F32))
            out_ref[rows, D_SSD + col:D_SSD + col + LANES] = (num / den * gate).astype(BF16)
    conv_next(3)

    for g in range(SSD_GROUPS):
        gl = slice(g * GROUP_W, (g + 1) * GROUP_W)
        y = (jnp.concatenate(y_diag[g], axis=1) + y_off_raw[g] * jnp.exp2(acs_x[:, gl])
             + dskip_ref[:, gl] * xs_g[g])
        yz = y * _silu(z_ref[rows, gl].astype(F32))
        ms = jnp.mean(yz * yz, axis=-1, keepdims=True)
        out_ref[rows, gl] = (yz * lax.rsqrt(ms + EPS) * nw_ref[:, gl]).astype(BF16)


def _meta_kernel(xs_ref, bc_ref, k_ref, v_ref, dt_ref, ck_ref, sk_ref,
                 convw_ref, convb_ref, dtb_ref, alog_ref, e_ref,
                 state_ref, tail_ref, kk_ref, vv_ref,
                 xbuf_ref, xact_ref, bact_ref, cact_ref):
    xbuf_ref[:, 0:8, :] = jnp.zeros((CONV_TILES, 8, LANES), F32)
    state_ref[...] = jnp.zeros(state_ref.shape, F32)
    for t in range(CONV_TILES):
        _conv_tile(t, xs_ref, bc_ref, 0, xbuf_ref, convw_ref, convb_ref, xact_ref, bact_ref, cact_ref)
    tail_ref[...] = xbuf_ref[:, 0:8, :]
    valid = jnp.where(_row_iota((CHUNK, LANES)) >= PAD_LEAD, 1.0, 0.0)
    dt, acs = _dt_cumsum(dt_ref, 0, dtb_ref, alog_ref, valid)
    dt_x, acs_x = _expand_heads(dt, acs, e_ref)
    for g in range(SSD_GROUPS):
        _state_update(g, xact_ref, bact_ref, dt_x, acs_x, state_ref)
    _rope_kv(k_ref, v_ref, 0, ck_ref, sk_ref, kk_ref, vv_ref, 0)


def _mixer_kernel(z_ref, xs_ref, bc_ref, q_ref, g_ref, k_ref, v_ref, dt_ref,
                  cq_ref, sq_ref, ck_ref, sk_ref,
                  state0_ref, tail0_ref, kk0_ref, vv0_ref,
                  convw_ref, convb_ref, dtb_ref, alog_ref, dskip_ref, nw_ref, e_ref, sinks_ref,
                  out_ref,
                  state_ref, xbuf_ref, xact_ref, bact_ref, cact_ref, xbd_ref, kk_ref, vv_ref, *, chunks):
    j = pl.program_id(1)

    @pl.when(j == 0)
    def _():
        state_ref[...] = state0_ref[...]
        xbuf_ref[:, 0:8, :] = tail0_ref[...]
        xbd_ref[...] = jnp.zeros(xbd_ref.shape, BF16)
        kk_ref[...] = jnp.zeros(kk_ref.shape, BF16)
        vv_ref[:, :, 0:LANES] = jnp.zeros((ATT_KV_HEADS, BAND, LANES), BF16)
        vv_ref[:, :, LANES:] = jnp.ones((ATT_KV_HEADS, BAND, LANES), BF16)
        kk_ref[:, 2 * CHUNK:BAND, :] = kk0_ref[...]
        vv_ref[:, 2 * CHUNK:BAND, 0:LANES] = vv0_ref[...]

    refs = (z_ref, q_ref, g_ref, k_ref, v_ref, dt_ref, cq_ref, sq_ref, ck_ref, sk_ref,
            dtb_ref, alog_ref, dskip_ref, nw_ref, e_ref, sinks_ref, out_ref,
            state_ref, xbd_ref, kk_ref, vv_ref)
    act = lambda s: (xact_ref.at[s], bact_ref.at[s], cact_ref.at[s])

    def conv_tiles(tiles, row0, slot):
        for t in tiles:
            _conv_tile(t, xs_ref, bc_ref, row0, xbuf_ref, convw_ref, convb_ref, *act(slot))

    conv_tiles(range(CONV_TILES), 0, 0)
    quarter = CONV_TILES // 4

    def body(ci, carry):
        row0 = pl.multiple_of(ci * CHUNK, CHUNK)
        cur = ci % 2
        next_row0 = pl.multiple_of(jnp.minimum(ci + 1, chunks - 1) * CHUNK, CHUNK)
        conv_next = lambda k: conv_tiles(range(k * quarter, (k + 1) * quarter), next_row0, 1 - cur)
        _chunk(refs, row0, j * chunks + ci + 1, act(cur), conv_next)
        return carry

    lax.fori_loop(0, chunks, body, 0)


def _const_spec(shape):
    nd = len(shape)
    return pl.BlockSpec(shape, lambda *_: (0,) * nd)


def _meta_state(projm, dtm, ckm, skm, conv_w, conv_b, dtb, alog, e_mat):
    col = lambda w, idx: pl.BlockSpec((CHUNK, w), lambda i: (0, idx))
    return pl.pallas_call(
        _meta_kernel,
        grid=(1,),
        in_specs=[
            col(D_SSD, 1), col(D_BC, 2), col(D_KV, 32), col(D_KV, 33),
            _const_spec((CHUNK, DT_PAD)), _const_spec((CHUNK, LANES)), _const_spec((CHUNK, LANES)),
            _const_spec((CONV_WIDTH, D_CONV)), _const_spec((1, D_CONV)),
            _const_spec((1, DT_PAD)), _const_spec((1, DT_PAD)), _const_spec((LANES, D_SSD)),
        ],
        out_specs=[
            _const_spec((SSD_GROUPS, D_STATE, GROUP_W)), _const_spec((CONV_TILES, 8, LANES)),
            _const_spec((ATT_KV_HEADS, CHUNK, LANES)), _const_spec((ATT_KV_HEADS, CHUNK, LANES)),
        ],
        out_shape=[
            jax.ShapeDtypeStruct((SSD_GROUPS, D_STATE, GROUP_W), F32),
            jax.ShapeDtypeStruct((CONV_TILES, 8, LANES), F32),
            jax.ShapeDtypeStruct((ATT_KV_HEADS, CHUNK, LANES), BF16),
            jax.ShapeDtypeStruct((ATT_KV_HEADS, CHUNK, LANES), BF16),
        ],
        scratch_shapes=[
            pltpu.VMEM((CONV_TILES, CHUNK + 8, LANES), F32),
            pltpu.VMEM((CHUNK, D_SSD), F32),
            pltpu.VMEM((CHUNK, D_BC // 2), BF16),
            pltpu.VMEM((CHUNK, D_BC // 2), BF16),
        ],
        compiler_params=pltpu.CompilerParams(
            dimension_semantics=("arbitrary",), vmem_limit_bytes=VMEM_LIMIT),
        name="meta_state",
    )(projm, projm, projm, projm, dtm, ckm, skm, conv_w, conv_b, dtb, alog, e_mat)


def _mixer(proj, dtr, tables, meta_state, params, *, batch, seq, tb):
    cq, sq, ck, sk = tables
    state0, tail0, kk0, vv0 = meta_state
    conv_w, conv_b, dtb, alog, dskip_x, ssd_nw, e_mat, sinks = params
    nblk = seq // tb
    rows = batch * seq
    chunks = tb // CHUNK
    col = lambda w, idx: pl.BlockSpec((tb, w), lambda b, j: (b * nblk + j, idx))
    tab = pl.BlockSpec((tb, LANES), lambda b, j: (j, 0))
    return pl.pallas_call(
        functools.partial(_mixer_kernel, chunks=chunks),
        grid=(batch, nblk),
        in_specs=[
            col(D_SSD, 0), col(D_SSD, 1), col(D_BC, 2), col(D_ATT, 6), col(D_ATT, 7),
            col(D_KV, 32), col(D_KV, 33), col(DT_PAD, 0),
            tab, tab, tab, tab,
            _const_spec((SSD_GROUPS, D_STATE, GROUP_W)), _const_spec((CONV_TILES, 8, LANES)),
            _const_spec((ATT_KV_HEADS, CHUNK, LANES)), _const_spec((ATT_KV_HEADS, CHUNK, LANES)),
            _const_spec((CONV_WIDTH, D_CONV)), _const_spec((1, D_CONV)),
            _const_spec((1, DT_PAD)), _const_spec((1, DT_PAD)),
            _const_spec((1, D_SSD)), _const_spec((1, D_SSD)), _const_spec((LANES, D_SSD)),
            pl.BlockSpec(memory_space=pltpu.SMEM),
        ],
        out_specs=pl.BlockSpec((tb, D_MIX), lambda b, j: (b * nblk + j, 0)),
        out_shape=jax.ShapeDtypeStruct((rows, D_MIX), BF16),
        scratch_shapes=[
            pltpu.VMEM((SSD_GROUPS, D_STATE, GROUP_W), F32),
            pltpu.VMEM((CONV_TILES, CHUNK + 8, LANES), F32),
            pltpu.VMEM((2, CHUNK, D_SSD), F32),
            pltpu.VMEM((2, CHUNK, D_BC // 2), BF16),
            pltpu.VMEM((2, CHUNK, D_BC // 2), BF16),
            pltpu.VMEM((SSD_HEADS // 2, 2 * CHUNK, LANES), BF16),
            pltpu.VMEM((ATT_KV_HEADS, BAND, LANES), BF16),
            pltpu.VMEM((ATT_KV_HEADS, BAND, 2 * LANES), BF16),
        ],
        compiler_params=pltpu.CompilerParams(
            dimension_semantics=("arbitrary", "arbitrary"), vmem_limit_bytes=VMEM_LIMIT),
        name="mixer",
    )(proj, proj, proj, proj, proj, proj, proj, dtr, cq, sq, ck, sk,
      state0, tail0, kk0, vv0, conv_w, conv_b, dtb, alog, dskip_x, ssd_nw, e_mat, sinks)


def _out_proj_kernel(mix_ref, w_ref, x_ref, nw_ref, o_ref):
    o = _dot(mix_ref[...], w_ref[...])
    ms = jnp.mean(o * o, axis=-1, keepdims=True)
    o_ref[...] = x_ref[...] + o * lax.rsqrt(ms + EPS) * nw_ref[...]


def _out_proj(mix, w_out, x2d, norm_w, *, tm):
    rows = mix.shape[0]
    return pl.pallas_call(
        _out_proj_kernel,
        grid=(rows // tm,),
        in_specs=[
            pl.BlockSpec((tm, D_MIX), lambda i: (i, 0)),
            pl.BlockSpec((D_MIX, D_MODEL), lambda i: (0, 0)),
            pl.BlockSpec((tm, D_MODEL), lambda i: (i, 0)),
            pl.BlockSpec((1, D_MODEL), lambda i: (0, 0)),
        ],
        out_specs=pl.BlockSpec((tm, D_MODEL), lambda i: (i, 0)),
        out_shape=jax.ShapeDtypeStruct((rows, D_MODEL), F32),
        compiler_params=pltpu.CompilerParams(
            dimension_semantics=("arbitrary",), vmem_limit_bytes=VMEM_LIMIT),
        name="out_proj",
    )(mix, w_out, x2d, norm_w)


def _pair_layout(w, heads):
    half = ATT_HEAD_DIM // 2
    w = w.reshape(w.shape[0], heads // 2, 2, 2, half)
    return w.transpose(0, 1, 3, 2, 4).reshape(w.shape[0], heads * ATT_HEAD_DIM)


def _rope_tables(n_pos):
    half = ATT_HEAD_DIM // 2
    pos = jnp.arange(n_pos, dtype=jnp.int32) - PAD_LEAD
    inv = ROPE_THETA ** (-jnp.arange(half, dtype=F32) / half)
    ang = pos.astype(F32)[:, None] * inv[None, :]
    cos, sin = jnp.cos(ang), jnp.sin(ang)
    cos4 = jnp.concatenate([cos, cos, cos, cos], axis=1)
    sin4 = jnp.concatenate([-sin, -sin, sin, sin], axis=1)
    return cos4, sin4


def kernel(x, meta_tokens, norm_pre_w, w_in, conv_w, conv_b, dt_bias, a_log, d_skip, ssd_norm_w,
           attn_sinks, w_out, norm_post_w):
    batch, seq, _ = x.shape
    assert norm_pre_w.shape[0] == 1 and seq % CHUNK == 0
    rows = batch * seq
    x2d = x.reshape(rows, D_MODEL)

    w = w_in[0]
    o = 0
    seg = {}
    for name, width in (("z", D_SSD), ("xs", D_SSD), ("bc", D_BC), ("dt", SSD_HEADS),
                        ("q", D_ATT), ("k", D_KV), ("v", D_KV), ("g", D_ATT)):
        seg[name] = w[:, o:o + width]
        o += width
    w_main = jnp.concatenate(
        [seg["z"], seg["xs"], seg["bc"], _pair_layout(seg["q"], ATT_Q_HEADS), seg["g"],
         _pair_layout(seg["k"], ATT_KV_HEADS), seg["v"]], axis=1).astype(BF16)
    w_dt = jnp.pad(seg["dt"], ((0, 0), (0, DT_PAD - SSD_HEADS))).astype(BF16)
    w_o = w_out[0].astype(BF16)

    pad_h = lambda v: jnp.pad(v.reshape(1, SSD_HEADS), ((0, 0), (0, DT_PAD - SSD_HEADS)))
    dtb = pad_h(dt_bias[0])
    alog = pad_h(a_log[0])
    dskip_x = jnp.repeat(d_skip[0], SSD_HEAD_DIM).reshape(1, D_SSD)
    e_rows = jnp.arange(LANES)
    e_mat = ((e_rows[:, None] % SSD_HEADS == jnp.arange(D_SSD)[None, :] // SSD_HEAD_DIM)
             & (e_rows[:, None] < 3 * SSD_HEADS)).astype(BF16)

    cos4, sin4 = _rope_tables(CHUNK + seq)
    scale = ATT_HEAD_DIM ** -0.5 * LOG2E
    tables = (cos4[CHUNK:] * scale, sin4[CHUNK:] * scale, cos4[CHUNK:], sin4[CHUNK:])

    npw = norm_pre_w[0].reshape(1, D_MODEL)
    cw, cb = conv_w[0], conv_b[0].reshape(1, D_CONV)

    xm = jnp.concatenate([jnp.zeros((PAD_LEAD, D_MODEL), x.dtype), meta_tokens.astype(x.dtype)], axis=0)
    projm, dtm = _in_proj(xm, npw, w_main, w_dt, tm=CHUNK, tn=2176)
    meta_state = _meta_state(projm, dtm, cos4[:CHUNK], sin4[:CHUNK], cw, cb, dtb, alog, e_mat)

    proj, dtr = _in_proj(x2d, npw, w_main, w_dt, tm=512, tn=2176)
    params = (cw, cb, dtb, alog, dskip_x, ssd_norm_w[0].reshape(1, D_SSD), e_mat, attn_sinks[0])
    mix = _mixer(proj, dtr, tables, meta_state, params, batch=batch, seq=seq, tb=512)
    out = _out_proj(mix, w_o, x2d, norm_post_w[0].reshape(1, D_MODEL), tm=256)
    return out.reshape(batch, seq, D_MODEL)
```

```python
import functools

import jax
import jax.numpy as jnp
from jax import lax
from jax.experimental import pallas as pl
from jax.experimental.pallas import tpu as pltpu

D_MODEL = 2048
CHUNK = 64
N_META = 16
PAD_LEAD = CHUNK - N_META
EPS = 1e-6

SSD_HEADS = 32
SSD_HEAD_DIM = 64
D_SSD = SSD_HEADS * SSD_HEAD_DIM
SSD_GROUPS = 8
D_STATE = 128
CONV_WIDTH = 4
D_BC = 2 * SSD_GROUPS * D_STATE
D_CONV = D_SSD + D_BC
GROUP_W = D_SSD // SSD_GROUPS

ATT_Q_HEADS = 16
ATT_KV_HEADS = 4
ATT_REP = ATT_Q_HEADS // ATT_KV_HEADS
ATT_HEAD_DIM = 64
D_ATT = ATT_Q_HEADS * ATT_HEAD_DIM
D_KV = ATT_KV_HEADS * ATT_HEAD_DIM
WINDOW_CHUNKS = 2
BAND = (WINDOW_CHUNKS + 1) * CHUNK
ROPE_THETA = 10000.0
D_MIX = D_SSD + D_ATT

LANES = 128
DT_PAD = LANES
D_PROJ_A = D_SSD + D_SSD + D_BC
D_PROJ_B = D_ATT + D_ATT + D_KV + D_KV
CONV_TILES = D_CONV // LANES
NEG_BIG = -1e30
LOG2E = 1.4426950408889634
VMEM_LIMIT = 56 * 1024 * 1024

F32 = jnp.float32
BF16 = jnp.bfloat16


def _dot(a, b):
    return jnp.dot(a, b, preferred_element_type=F32)


def _dot_nt(a, b):
    return lax.dot_general(a, b, (((1,), (1,)), ((), ())), preferred_element_type=F32)


def _dot_tn(a, b):
    return lax.dot_general(a, b, (((0,), (0,)), ((), ())), preferred_element_type=F32)


def _silu(x):
    return x * (1.0 / (1.0 + jnp.exp(-x)))


def _split_bf16(x, parts):
    out = []
    r = x
    for _ in range(parts):
        p = r.astype(BF16)
        out.append(p)
        r = r - p.astype(F32)
    return out


def _lane_iota(shape):
    return lax.broadcasted_iota(jnp.int32, shape, 1)


def _row_iota(shape):
    return lax.broadcasted_iota(jnp.int32, shape, 0)


def _in_proj_a_kernel(x0_ref, xnext_ref, nw_ref, w_ref, pa_ref, xn_out_ref, xn_ref, *, strip):
    i = pl.program_id(0)
    j = pl.program_id(1)
    tm = xnext_ref.shape[0]

    def norm_rows(src_ref, r0, nrows, slot):
        xv = src_ref[pl.ds(r0, nrows), :]
        ms = jnp.mean(xv * xv, axis=-1, keepdims=True)
        xn_ref[slot, pl.ds(r0, nrows), :] = (xv * lax.rsqrt(ms + EPS) * nw_ref[...]).astype(BF16)

    @pl.when((i == 0) & (j == 0))
    def _():
        def body(s, carry):
            norm_rows(x0_ref, pl.multiple_of(s * CHUNK, CHUNK), CHUNK, 0)
            return carry
        lax.fori_loop(0, tm // CHUNK, body, 0)

    cur = i % 2
    xn_cur = xn_ref[cur]
    xn_out_ref[...] = xn_cur
    pa_ref[...] = _dot(xn_cur, w_ref[...]).astype(BF16)
    r0 = pl.multiple_of(jnp.minimum(j * strip, tm - strip), CHUNK)
    norm_rows(xnext_ref, r0, strip, 1 - cur)


def _in_proj_a(x2d, norm_w, w_a, *, tm, tn):
    rows = x2d.shape[0]
    nrt, nct = rows // tm, D_PROJ_A // tn
    strip = min(tm, -(-tm // nct // CHUNK) * CHUNK)
    assert strip * nct >= tm and strip <= tm and tm % CHUNK == 0
    last = nrt - 1
    return pl.pallas_call(
        functools.partial(_in_proj_a_kernel, strip=strip),
        grid=(nrt, nct),
        in_specs=[
            pl.BlockSpec((tm, D_MODEL), lambda i, j: (0, 0)),
            pl.BlockSpec((tm, D_MODEL), lambda i, j: (jnp.minimum(i + 1, last), 0)),
            pl.BlockSpec((1, D_MODEL), lambda i, j: (0, 0)),
            pl.BlockSpec((D_MODEL, tn), lambda i, j: (0, j)),
        ],
        out_specs=[
            pl.BlockSpec((tm, tn), lambda i, j: (i, j)),
            pl.BlockSpec((tm, D_MODEL), lambda i, j: (i, 0)),
        ],
        out_shape=[
            jax.ShapeDtypeStruct((rows, D_PROJ_A), BF16),
            jax.ShapeDtypeStruct((rows, D_MODEL), BF16),
        ],
        scratch_shapes=[pltpu.VMEM((2, tm, D_MODEL), BF16)],
        compiler_params=pltpu.CompilerParams(
            dimension_semantics=("arbitrary", "arbitrary"), vmem_limit_bytes=VMEM_LIMIT),
        name="in_proj_a",
    )(x2d, x2d, norm_w, w_a)


def _in_proj_b_kernel(xn_ref, wb_ref, wdt_ref, pb_ref, dt_ref):
    xn = xn_ref[...]
    pb_ref[...] = _dot(xn, wb_ref[...]).astype(BF16)
    dt_ref[...] = _dot(xn, wdt_ref[...])


def _in_proj_b(xn, w_b, w_dt, *, tm):
    rows = xn.shape[0]
    return pl.pallas_call(
        _in_proj_b_kernel,
        grid=(rows // tm,),
        in_specs=[
            pl.BlockSpec((tm, D_MODEL), lambda i: (i, 0)),
            pl.BlockSpec((D_MODEL, D_PROJ_B), lambda i: (0, 0)),
            pl.BlockSpec((D_MODEL, DT_PAD), lambda i: (0, 0)),
        ],
        out_specs=[
            pl.BlockSpec((tm, D_PROJ_B), lambda i: (i, 0)),
            pl.BlockSpec((tm, DT_PAD), lambda i: (i, 0)),
        ],
        out_shape=[
            jax.ShapeDtypeStruct((rows, D_PROJ_B), BF16),
            jax.ShapeDtypeStruct((rows, DT_PAD), F32),
        ],
        compiler_params=pltpu.CompilerParams(
            dimension_semantics=("arbitrary",), vmem_limit_bytes=VMEM_LIMIT),
        name="in_proj_b",
    )(xn, w_b, w_dt)


def _conv_tile(t, xs_ref, bc_ref, row0, xbuf_ref, convw_ref, convb_ref, xact_ref, bact_ref, cact_ref):
    rows = pl.ds(row0, CHUNK)
    cols = slice(t * LANES, (t + 1) * LANES)
    if t < D_SSD // LANES:
        raw = xs_ref[rows, cols]
    else:
        raw = bc_ref[rows, t * LANES - D_SSD:(t + 1) * LANES - D_SSD]
    xbuf_ref[t, 8:8 + CHUNK, :] = raw.astype(F32)
    acc = convb_ref[:, cols] + convw_ref[0:1, cols] * xbuf_ref[t, 5:5 + CHUNK, :]
    for i in range(1, CONV_WIDTH):
        acc = acc + convw_ref[i:i + 1, cols] * xbuf_ref[t, 5 + i:5 + i + CHUNK, :]
    act = _silu(acc)
    if t < D_SSD // LANES:
        xact_ref[:, cols] = act
    elif t < (D_SSD + D_BC // 2) // LANES:
        bact_ref[:, t * LANES - D_SSD:(t + 1) * LANES - D_SSD] = act.astype(BF16)
    else:
        o = t * LANES - D_SSD - D_BC // 2
        cact_ref[:, o:o + LANES] = act.astype(BF16)
    xbuf_ref[t, 0:8, :] = xbuf_ref[t, CHUNK:CHUNK + 8, :]


def _dt_cumsum(dt_ref, row0, dtb_ref, alog_ref, valid):
    dtr = dt_ref[pl.ds(row0, CHUNK), :]
    xx = dtr + dtb_ref[...]
    dt = jnp.maximum(xx, 0.0) + jnp.log1p(jnp.exp(-jnp.abs(xx)))
    if valid is not None:
        dt = dt * valid
    dt = jnp.where(_lane_iota((CHUNK, LANES)) < SSD_HEADS, dt, 0.0)
    dta = dt * (-jnp.exp(alog_ref[...]) * LOG2E)
    rr = _row_iota((CHUNK, 3 * CHUNK))
    cc = _lane_iota((CHUNK, 3 * CHUNK)) & (CHUNK - 1)
    tril3 = jnp.where(rr >= cc, 1.0, 0.0).astype(BF16)
    acs = _dot(tril3, jnp.concatenate(_split_bf16(dta, 3), axis=0))
    return dt, acs


def _expand_heads(dt, acs, e_ref):
    a_hi, a_mid, a_lo = [p.astype(F32) for p in _split_bf16(acs, 3)]
    a_stack = (a_hi + pltpu.roll(a_mid, 32, 1) + pltpu.roll(a_lo, 64, 1)).astype(BF16)
    acs_x = _dot(a_stack, e_ref[...])
    d_hi, d_mid = [p.astype(F32) for p in _split_bf16(dt, 2)]
    d_stack = (d_hi + pltpu.roll(d_mid, 32, 1)).astype(BF16)
    dt_x = _dot(d_stack, e_ref[...])
    return dt_x, acs_x


def _state_update(g, xact_ref, bact_ref, dt_x, acs_x, state_ref):
    gl = slice(g * GROUP_W, (g + 1) * GROUP_W)
    acs_g = acs_x[:, gl]
    a_last = acs_g[CHUNK - 1:CHUNK, :]
    x_g = xact_ref[:, gl]
    x_dt = x_g * dt_x[:, gl]
    xw = (x_dt * jnp.exp2(a_last - acs_g)).astype(BF16)
    b_g = bact_ref[:, g * D_STATE:(g + 1) * D_STATE]
    state_ref[g] = state_ref[g] * jnp.exp2(a_last) + _dot_tn(b_g, xw)
    return x_g, x_dt


def _rope_kv(k_ref, v_ref, row0, ck_ref, sk_ref, kk_ref, vv_ref, slot):
    rows = pl.ds(row0, CHUNK)
    lane = _lane_iota((CHUNK, LANES))
    first_of_pair = (lane & 63) < 32
    low_half = lane < CHUNK
    ck = ck_ref[rows, :]
    sk = sk_ref[rows, :]
    dst = slice(slot * CHUNK, (slot + 1) * CHUNK)
    for c in range(2):
        kx = k_ref[rows, c * LANES:(c + 1) * LANES].astype(F32)
        kr = kx * ck + pltpu.roll(kx, 64, 1) * sk
        kk_ref[2 * c, dst, :] = jnp.where(first_of_pair, kr, pltpu.roll(kr, 32, 1)).astype(BF16)
        kk_ref[2 * c + 1, dst, :] = jnp.where(first_of_pair, pltpu.roll(kr, 96, 1), kr).astype(BF16)
        vx = v_ref[rows, c * LANES:(c + 1) * LANES].astype(F32)
        vr = pltpu.roll(vx, 64, 1)
        vv_ref[2 * c, dst, 0:LANES] = jnp.where(low_half, vx, vr).astype(BF16)
        vv_ref[2 * c + 1, dst, 0:LANES] = jnp.where(low_half, vr, vx).astype(BF16)


def _chunk(refs, row0, chunk_idx, act, conv_next):
    (z_ref, q_ref, g_ref, k_ref, v_ref, dt_ref, cq_ref, sq_ref, ck_ref, sk_ref,
     dtb_ref, alog_ref, dskip_ref, nw_ref, e_ref, sinks_ref, out_ref,
     state_ref, xbd_ref, kk_ref, vv_ref) = refs
    xact_ref, bact_ref, cact_ref = act
    rows = pl.ds(row0, CHUNK)
    lane = _lane_iota((CHUNK, LANES))
    row = _row_iota((CHUNK, LANES))
    first_of_pair = (lane & 63) < 32
    low_half = lane < CHUNK
    s_idx = lane & (CHUNK - 1)
    diag = row == s_idx
    causal = row >= s_idx

    dt, acs = _dt_cumsum(dt_ref, row0, dtb_ref, alog_ref, None)

    kk_ref[:, 0:2 * CHUNK, :] = kk_ref[:, CHUNK:BAND, :]
    vv_ref[:, 0:2 * CHUNK, 0:LANES] = vv_ref[:, CHUNK:BAND, 0:LANES]
    _rope_kv(k_ref, v_ref, row0, ck_ref, sk_ref, kk_ref, vv_ref, WINDOW_CHUNKS)
    key_abs = (chunk_idx - WINDOW_CHUNKS) * CHUNK + _lane_iota((1, BAND))
    bias = jnp.where(key_abs >= PAD_LEAD, 0.0, NEG_BIG)
    cq = cq_ref[rows, :]
    sq = sq_ref[rows, :]
    scores = []
    for kv in range(ATT_KV_HEADS):
        q_rows = []
        for c in (2 * kv, 2 * kv + 1):
            qx = q_ref[rows, c * LANES:(c + 1) * LANES].astype(F32)
            qr = qx * cq + pltpu.roll(qx, 64, 1) * sq
            q_rows.append(jnp.where(first_of_pair, qr, 0.0).astype(BF16))
            q_rows.append(jnp.where(first_of_pair, 0.0, qr).astype(BF16))
        q_stack = jnp.concatenate(q_rows, axis=0)
        scores.append(_dot_nt(q_stack, kk_ref[kv]) + bias)

    cb2, y_off_raw = [], []
    for g in range(SSD_GROUPS):
        nl = slice(g * D_STATE, (g + 1) * D_STATE)
        b_g = bact_ref[:, nl]
        c_g = cact_ref[:, nl]
        cb2.append(_dot_nt(c_g, jnp.concatenate([b_g, b_g], axis=0)))
        y_off_raw.append(_dot(c_g, state_ref[g].astype(BF16)))
    conv_next(0)

    dt_x, acs_x = _expand_heads(dt, acs, e_ref)

    pvs, e_sinks = [], []
    for kv in range(ATT_KV_HEADS):
        s = scores[kv]
        p_rows, e_sink = [], []
        for r in range(ATT_REP):
            s_r = s[r * CHUNK:(r + 1) * CHUNK, :]
            sink = sinks_ref[kv * ATT_REP + r] * LOG2E
            m = jnp.maximum(jnp.max(s_r, axis=-1, keepdims=True), sink)
            p_rows.append(jnp.exp2(s_r - m).astype(BF16))
            e_sink.append(jnp.exp2(sink - m))
        pvs.append(_dot(jnp.concatenate(p_rows, axis=0), vv_ref[kv]))
        e_sinks.append(e_sink)
    conv_next(1)

    xs_g, y_diag = [], []
    for g in range(SSD_GROUPS):
        x_g, x_dt = _state_update(g, xact_ref, bact_ref, dt_x, acs_x, state_ref)
        xs_g.append(x_g)
        parts = []
        for pp in range(2):
            pls = slice(g * GROUP_W + pp * LANES, g * GROUP_W + (pp + 1) * LANES)
            col = acs_x[:, pls]
            rowv = jnp.sum(jnp.where(diag, col, 0.0), axis=0, keepdims=True)
            seg = jnp.where(causal, col - rowv, NEG_BIG)
            lmat = (jnp.exp2(seg) * cb2[g]).astype(BF16)
            xp = x_dt[:, pp * LANES:(pp + 1) * LANES].astype(BF16)
            p = 2 * g + pp
            xbd_ref[p, 0:CHUNK, 0:CHUNK] = xp[:, 0:CHUNK]
            xbd_ref[p, CHUNK:, CHUNK:] = xp[:, CHUNK:]
            parts.append(_dot(lmat, xbd_ref[p]))
        y_diag.append(parts)
    conv_next(2)

    for kv in range(ATT_KV_HEADS):
        pv = pvs[kv]
        for pr in range(2):
            r0, r1 = 2 * pr, 2 * pr + 1
            blk0 = pv[r0 * CHUNK:(r0 + 1) * CHUNK, :]
            blk1 = pv[r1 * CHUNK:(r1 + 1) * CHUNK, :]
            num = jnp.where(low_half, blk0[:, 0:LANES], blk1[:, 0:LANES])
            den = jnp.where(low_half, blk0[:, LANES:] + e_sinks[kv][r0], blk1[:, LANES:] + e_sinks[kv][r1])
            col = (2 * kv + pr) * LANES
            gate = _silu(g_ref[rows, col:col + LANES].astype(F32))
            out_ref[rows, D_SSD + col:D_SSD + col + LANES] = (num / den * gate).astype(BF16)
    conv_next(3)

    for g in range(SSD_GROUPS):
        gl = slice(g * GROUP_W, (g + 1) * GROUP_W)
        y = (jnp.concatenate(y_diag[g], axis=1) + y_off_raw[g] * jnp.exp2(acs_x[:, gl])
             + dskip_ref[:, gl] * xs_g[g])
        yz = y * _silu(z_ref[rows, gl].astype(F32))
        ms = jnp.mean(yz * yz, axis=-1, keepdims=True)
        out_ref[rows, gl] = (yz * lax.rsqrt(ms + EPS) * nw_ref[:, gl]).astype(BF16)


def _meta_kernel(xs_ref, bc_ref, k_ref, v_ref, dt_ref, ck_ref, sk_ref,
                 convw_ref, convb_ref, dtb_ref, alog_ref, e_ref,
                 state_ref, tail_ref, kk_ref, vv_ref,
                 xbuf_ref, xact_ref, bact_ref, cact_ref):
    xbuf_ref[:, 0:8, :] = jnp.zeros((CONV_TILES, 8, LANES), F32)
    state_ref[...] = jnp.zeros(state_ref.shape, F32)
    for t in range(CONV_TILES):
        _conv_tile(t, xs_ref, bc_ref, 0, xbuf_ref, convw_ref, convb_ref, xact_ref, bact_ref, cact_ref)
    tail_ref[...] = xbuf_ref[:, 0:8, :]
    valid = jnp.where(_row_iota((CHUNK, LANES)) >= PAD_LEAD, 1.0, 0.0)
    dt, acs = _dt_cumsum(dt_ref, 0, dtb_ref, alog_ref, valid)
    dt_x, acs_x = _expand_heads(dt, acs, e_ref)
    for g in range(SSD_GROUPS):
        _state_update(g, xact_ref, bact_ref, dt_x, acs_x, state_ref)
    _rope_kv(k_ref, v_ref, 0, ck_ref, sk_ref, kk_ref, vv_ref, 0)


def _mixer_kernel(z_ref, xs_ref, bc_ref, q_ref, g_ref, k_ref, v_ref, dt_ref,
                  cq_ref, sq_ref, ck_ref, sk_ref,
                  state0_ref, tail0_ref, kk0_ref, vv0_ref,
                  convw_ref, convb_ref, dtb_ref, alog_ref, dskip_ref, nw_ref, e_ref, sinks_ref,
                  out_ref,
                  state_ref, xbuf_ref, xact_ref, bact_ref, cact_ref, xbd_ref, kk_ref, vv_ref, *, chunks):
    j = pl.program_id(1)

    @pl.when(j == 0)
    def _():
        state_ref[...] = state0_ref[...]
        xbuf_ref[:, 0:8, :] = tail0_ref[...]
        xbd_ref[...] = jnp.zeros(xbd_ref.shape, BF16)
        kk_ref[...] = jnp.zeros(kk_ref.shape, BF16)
        vv_ref[:, :, 0:LANES] = jnp.zeros((ATT_KV_HEADS, BAND, LANES), BF16)
        vv_ref[:, :, LANES:] = jnp.ones((ATT_KV_HEADS, BAND, LANES), BF16)
        kk_ref[:, 2 * CHUNK:BAND, :] = kk0_ref[...]
        vv_ref[:, 2 * CHUNK:BAND, 0:LANES] = vv0_ref[...]

    refs = (z_ref, q_ref, g_ref, k_ref, v_ref, dt_ref, cq_ref, sq_ref, ck_ref, sk_ref,
            dtb_ref, alog_ref, dskip_ref, nw_ref, e_ref, sinks_ref, out_ref,
            state_ref, xbd_ref, kk_ref, vv_ref)
    act = lambda s: (xact_ref.at[s], bact_ref.at[s], cact_ref.at[s])

    def conv_tiles(tiles, row0, slot):
        for t in tiles:
            _conv_tile(t, xs_ref, bc_ref, row0, xbuf_ref, convw_ref, convb_ref, *act(slot))

    conv_tiles(range(CONV_TILES), 0, 0)
    quarter = CONV_TILES // 4

    def body(ci, carry):
        row0 = pl.multiple_of(ci * CHUNK, CHUNK)
        cur = ci % 2
        next_row0 = pl.multiple_of(jnp.minimum(ci + 1, chunks - 1) * CHUNK, CHUNK)
        conv_next = lambda k: conv_tiles(range(k * quarter, (k + 1) * quarter), next_row0, 1 - cur)
        _chunk(refs, row0, j * chunks + ci + 1, act(cur), conv_next)
        return carry

    lax.fori_loop(0, chunks, body, 0)


def _const_spec(shape):
    nd = len(shape)
    return pl.BlockSpec(shape, lambda *_: (0,) * nd)


def _meta_state(projm_a, projm_b, dtm, ckm, skm, conv_w, conv_b, dtb, alog, e_mat):
    col = lambda w, idx: pl.BlockSpec((CHUNK, w), lambda i: (0, idx))
    return pl.pallas_call(
        _meta_kernel,
        grid=(1,),
        in_specs=[
            col(D_SSD, 1), col(D_BC, 2), col(D_KV, 2 * D_ATT // D_KV), col(D_KV, 2 * D_ATT // D_KV + 1),
            _const_spec((CHUNK, DT_PAD)), _const_spec((CHUNK, LANES)), _const_spec((CHUNK, LANES)),
            _const_spec((CONV_WIDTH, D_CONV)), _const_spec((1, D_CONV)),
            _const_spec((1, DT_PAD)), _const_spec((1, DT_PAD)), _const_spec((LANES, D_SSD)),
        ],
        out_specs=[
            _const_spec((SSD_GROUPS, D_STATE, GROUP_W)), _const_spec((CONV_TILES, 8, LANES)),
            _const_spec((ATT_KV_HEADS, CHUNK, LANES)), _const_spec((ATT_KV_HEADS, CHUNK, LANES)),
        ],
        out_shape=[
            jax.ShapeDtypeStruct((SSD_GROUPS, D_STATE, GROUP_W), F32),
            jax.ShapeDtypeStruct((CONV_TILES, 8, LANES), F32),
            jax.ShapeDtypeStruct((ATT_KV_HEADS, CHUNK, LANES), BF16),
            jax.ShapeDtypeStruct((ATT_KV_HEADS, CHUNK, LANES), BF16),
        ],
        scratch_shapes=[
            pltpu.VMEM((CONV_TILES, CHUNK + 8, LANES), F32),
            pltpu.VMEM((CHUNK, D_SSD), F32),
            pltpu.VMEM((CHUNK, D_BC // 2), BF16),
            pltpu.VMEM((CHUNK, D_BC // 2), BF16),
        ],
        compiler_params=pltpu.CompilerParams(
            dimension_semantics=("arbitrary",), vmem_limit_bytes=VMEM_LIMIT),
        name="meta_state",
    )(projm_a, projm_a, projm_b, projm_b, dtm, ckm, skm, conv_w, conv_b, dtb, alog, e_mat)


def _mixer(proj_a, proj_b, dtr, tables, meta_state, params, *, batch, seq, tb):
    cq, sq, ck, sk = tables
    state0, tail0, kk0, vv0 = meta_state
    conv_w, conv_b, dtb, alog, dskip_x, ssd_nw, e_mat, sinks = params
    nblk = seq // tb
    rows = batch * seq
    chunks = tb // CHUNK
    col = lambda w, idx: pl.BlockSpec((tb, w), lambda b, j: (b * nblk + j, idx))
    tab = pl.BlockSpec((tb, LANES), lambda b, j: (j, 0))
    return pl.pallas_call(
        functools.partial(_mixer_kernel, chunks=chunks),
        grid=(batch, nblk),
        in_specs=[
            col(D_SSD, 0), col(D_SSD, 1), col(D_BC, 2), col(D_ATT, 0), col(D_ATT, 1),
            col(D_KV, 2 * D_ATT // D_KV), col(D_KV, 2 * D_ATT // D_KV + 1), col(DT_PAD, 0),
            tab, tab, tab, tab,
            _const_spec((SSD_GROUPS, D_STATE, GROUP_W)), _const_spec((CONV_TILES, 8, LANES)),
            _const_spec((ATT_KV_HEADS, CHUNK, LANES)), _const_spec((ATT_KV_HEADS, CHUNK, LANES)),
            _const_spec((CONV_WIDTH, D_CONV)), _const_spec((1, D_CONV)),
            _const_spec((1, DT_PAD)), _const_spec((1, DT_PAD)),
            _const_spec((1, D_SSD)), _const_spec((1, D_SSD)), _const_spec((LANES, D_SSD)),
            pl.BlockSpec(memory_space=pltpu.SMEM),
        ],
        out_specs=pl.BlockSpec((tb, D_MIX), lambda b, j: (b * nblk + j, 0)),
        out_shape=jax.ShapeDtypeStruct((rows, D_MIX), BF16),
        scratch_shapes=[
            pltpu.VMEM((SSD_GROUPS, D_STATE, GROUP_W), F32),
            pltpu.VMEM((CONV_TILES, CHUNK + 8, LANES), F32),
            pltpu.VMEM((2, CHUNK, D_SSD), F32),
            pltpu.VMEM((2, CHUNK, D_BC // 2), BF16),
            pltpu.VMEM((2, CHUNK, D_BC // 2), BF16),
            pltpu.VMEM((SSD_HEADS // 2, 2 * CHUNK, LANES), BF16),
            pltpu.VMEM((ATT_KV_HEADS, BAND, LANES), BF16),
            pltpu.VMEM((ATT_KV_HEADS, BAND, 2 * LANES), BF16),
        ],
        compiler_params=pltpu.CompilerParams(
            dimension_semantics=("arbitrary", "arbitrary"), vmem_limit_bytes=VMEM_LIMIT),
        name="mixer",
    )(proj_a, proj_a, proj_a, proj_b, proj_b, proj_b, proj_b, dtr, cq, sq, ck, sk,
      state0, tail0, kk0, vv0, conv_w, conv_b, dtb, alog, dskip_x, ssd_nw, e_mat, sinks)


def _out_proj_kernel(mix_ref, w_ref, x_ref, nw_ref, o_ref):
    o = _dot(mix_ref[...], w_ref[...])
    ms = jnp.mean(o * o, axis=-1, keepdims=True)
    o_ref[...] = x_ref[...] + o * lax.rsqrt(ms + EPS) * nw_ref[...]


def _out_proj(mix, w_out, x2d, norm_w, *, tm):
    rows = mix.shape[0]
    return pl.pallas_call(
        _out_proj_kernel,
        grid=(rows // tm,),
        in_specs=[
            pl.BlockSpec((tm, D_MIX), lambda i: (i, 0)),
            pl.BlockSpec((D_MIX, D_MODEL), lambda i: (0, 0)),
            pl.BlockSpec((tm, D_MODEL), lambda i: (i, 0)),
            pl.BlockSpec((1, D_MODEL), lambda i: (0, 0)),
        ],
        out_specs=pl.BlockSpec((tm, D_MODEL), lambda i: (i, 0)),
        out_shape=jax.ShapeDtypeStruct((rows, D_MODEL), F32),
        compiler_params=pltpu.CompilerParams(
            dimension_semantics=("arbitrary",), vmem_limit_bytes=VMEM_LIMIT),
        name="out_proj",
    )(mix, w_out, x2d, norm_w)


def _pair_layout(w, heads):
    half = ATT_HEAD_DIM // 2
    w = w.reshape(w.shape[0], heads // 2, 2, 2, half)
    return w.transpose(0, 1, 3, 2, 4).reshape(w.shape[0], heads * ATT_HEAD_DIM)


def _rope_tables(n_pos):
    half = ATT_HEAD_DIM // 2
    pos = jnp.arange(n_pos, dtype=jnp.int32) - PAD_LEAD
    inv = ROPE_THETA ** (-jnp.arange(half, dtype=F32) / half)
    ang = pos.astype(F32)[:, None] * inv[None, :]
    cos, sin = jnp.cos(ang), jnp.sin(ang)
    cos4 = jnp.concatenate([cos, cos, cos, cos], axis=1)
    sin4 = jnp.concatenate([-sin, -sin, sin, sin], axis=1)
    return cos4, sin4


def kernel(x, meta_tokens, norm_pre_w, w_in, conv_w, conv_b, dt_bias, a_log, d_skip, ssd_norm_w,
           attn_sinks, w_out, norm_post_w):
    batch, seq, _ = x.shape
    assert norm_pre_w.shape[0] == 1 and seq % CHUNK == 0
    rows = batch * seq
    x2d = x.reshape(rows, D_MODEL)

    w = w_in[0]
    w_a = w[:, :D_PROJ_A].astype(BF16)
    o = D_PROJ_A
    seg = {}
    for name, width in (("dt", SSD_HEADS), ("q", D_ATT), ("k", D_KV), ("v", D_KV), ("g", D_ATT)):
        seg[name] = w[:, o:o + width].astype(BF16)
        o += width
    w_b = jnp.concatenate([_pair_layout(seg["q"], ATT_Q_HEADS), seg["g"],
                           _pair_layout(seg["k"], ATT_KV_HEADS), seg["v"]], axis=1)
    w_dt = jnp.pad(seg["dt"], ((0, 0), (0, DT_PAD - SSD_HEADS)))
    w_o = w_out[0].astype(BF16)

    pad_h = lambda v: jnp.pad(v.reshape(1, SSD_HEADS), ((0, 0), (0, DT_PAD - SSD_HEADS)))
    dtb = pad_h(dt_bias[0])
    alog = pad_h(a_log[0])
    dskip_x = jnp.repeat(d_skip[0], SSD_HEAD_DIM).reshape(1, D_SSD)
    e_rows = jnp.arange(LANES)
    e_mat = ((e_rows[:, None] % SSD_HEADS == jnp.arange(D_SSD)[None, :] // SSD_HEAD_DIM)
             & (e_rows[:, None] < 3 * SSD_HEADS)).astype(BF16)

    cos4, sin4 = _rope_tables(CHUNK + seq)
    scale = ATT_HEAD_DIM ** -0.5 * LOG2E
    tables = (cos4[CHUNK:] * scale, sin4[CHUNK:] * scale, cos4[CHUNK:], sin4[CHUNK:])

    npw = norm_pre_w[0].reshape(1, D_MODEL)
    cw, cb = conv_w[0], conv_b[0].reshape(1, D_CONV)

    xm = jnp.concatenate([jnp.zeros((PAD_LEAD, D_MODEL), x.dtype), meta_tokens.astype(x.dtype)], axis=0)
    projm_a, xnm = _in_proj_a(xm, npw, w_a, tm=CHUNK, tn=2048)
    projm_b, dtm = _in_proj_b(xnm, w_b, w_dt, tm=CHUNK)
    meta_state = _meta_state(projm_a, projm_b, dtm, cos4[:CHUNK], sin4[:CHUNK], cw, cb, dtb, alog, e_mat)

    proj_a, xn = _in_proj_a(x2d, npw, w_a, tm=512, tn=2048)
    proj_b, dtr = _in_proj_b(xn, w_b, w_dt, tm=1024)
    params = (cw, cb, dtb, alog, dskip_x, ssd_norm_w[0].reshape(1, D_SSD), e_mat, attn_sinks[0])
    mix = _mixer(proj_a, proj_b, dtr, tables, meta_state, params, batch=batch, seq=seq, tb=512)
    out = _out_proj(mix, w_o, x2d, norm_post_w[0].reshape(1, D_MODEL), tm=256)
    return out.reshape(batch, seq, D_MODEL)
```

```python
import functools

import jax
import jax.numpy as jnp
from jax import lax
from jax.experimental import pallas as pl
from jax.experimental.pallas import tpu as pltpu

D_MODEL = 2048
CHUNK = 64
N_META = 16
PAD_LEAD = CHUNK - N_META
EPS = 1e-6

SSD_HEADS = 32
SSD_HEAD_DIM = 64
D_SSD = SSD_HEADS * SSD_HEAD_DIM
SSD_GROUPS = 8
D_STATE = 128
CONV_WIDTH = 4
D_BC = 2 * SSD_GROUPS * D_STATE
D_CONV = D_SSD + D_BC
GROUP_W = D_SSD // SSD_GROUPS

ATT_Q_HEADS = 16
ATT_KV_HEADS = 4
ATT_REP = ATT_Q_HEADS // ATT_KV_HEADS
ATT_HEAD_DIM = 64
D_ATT = ATT_Q_HEADS * ATT_HEAD_DIM
D_KV = ATT_KV_HEADS * ATT_HEAD_DIM
WINDOW_CHUNKS = 2
BAND = (WINDOW_CHUNKS + 1) * CHUNK
ROPE_THETA = 10000.0
D_MIX = D_SSD + D_ATT

LANES = 128
DT_PAD = LANES
D_PROJ_A = D_SSD + D_SSD + D_BC
D_PROJ_B = D_ATT + D_ATT + D_KV + D_KV
CONV_TILES = D_CONV // LANES
NEG_BIG = -1e30
LOG2E = 1.4426950408889634
VMEM_LIMIT = 56 * 1024 * 1024

F32 = jnp.float32
BF16 = jnp.bfloat16


def _dot(a, b):
    return jnp.dot(a, b, preferred_element_type=F32)


def _dot_nt(a, b):
    return lax.dot_general(a, b, (((1,), (1,)), ((), ())), preferred_element_type=F32)


def _dot_tn(a, b):
    return lax.dot_general(a, b, (((0,), (0,)), ((), ())), preferred_element_type=F32)


def _silu(x):
    return x * (1.0 / (1.0 + jnp.exp(-x)))


def _split_bf16(x, parts):
    out = []
    r = x
    for _ in range(parts):
        p = r.astype(BF16)
        out.append(p)
        r = r - p.astype(F32)
    return out


def _lane_iota(shape):
    return lax.broadcasted_iota(jnp.int32, shape, 1)


def _row_iota(shape):
    return lax.broadcasted_iota(jnp.int32, shape, 0)


def _in_proj_a_kernel(x0_ref, xnext_ref, nw_ref, w_ref, pa_ref, xn_out_ref, xn_ref, *, strip):
    i = pl.program_id(0)
    j = pl.program_id(1)
    tm = xnext_ref.shape[0]

    def norm_rows(src_ref, r0, nrows, slot):
        xv = src_ref[pl.ds(r0, nrows), :]
        ms = jnp.mean(xv * xv, axis=-1, keepdims=True)
        xn_ref[slot, pl.ds(r0, nrows), :] = (xv * lax.rsqrt(ms + EPS) * nw_ref[...]).astype(BF16)

    @pl.when((i == 0) & (j == 0))
    def _():
        def body(s, carry):
            norm_rows(x0_ref, pl.multiple_of(s * CHUNK, CHUNK), CHUNK, 0)
            return carry
        lax.fori_loop(0, tm // CHUNK, body, 0)

    cur = i % 2
    xn_cur = xn_ref[cur]
    xn_out_ref[...] = xn_cur
    pa_ref[...] = _dot_nt(xn_cur, w_ref[...]).astype(BF16)
    r0 = pl.multiple_of(jnp.minimum(j * strip, tm - strip), CHUNK)
    norm_rows(xnext_ref, r0, strip, 1 - cur)


def _in_proj_a(x2d, norm_w, w_t, *, tm, tn):
    rows = x2d.shape[0]
    nrt, nct = rows // tm, D_PROJ_A // tn
    strip = min(tm, -(-tm // nct // CHUNK) * CHUNK)
    assert strip * nct >= tm and strip <= tm and tm % CHUNK == 0
    last = nrt - 1
    return pl.pallas_call(
        functools.partial(_in_proj_a_kernel, strip=strip),
        grid=(nrt, nct),
        in_specs=[
            pl.BlockSpec((tm, D_MODEL), lambda i, j: (0, 0)),
            pl.BlockSpec((tm, D_MODEL), lambda i, j: (jnp.minimum(i + 1, last), 0)),
            pl.BlockSpec((1, D_MODEL), lambda i, j: (0, 0)),
            pl.BlockSpec((tn, D_MODEL), lambda i, j: (j, 0)),
        ],
        out_specs=[
            pl.BlockSpec((tm, tn), lambda i, j: (i, j)),
            pl.BlockSpec((tm, D_MODEL), lambda i, j: (i, 0)),
        ],
        out_shape=[
            jax.ShapeDtypeStruct((rows, D_PROJ_A), BF16),
            jax.ShapeDtypeStruct((rows, D_MODEL), BF16),
        ],
        scratch_shapes=[pltpu.VMEM((2, tm, D_MODEL), BF16)],
        compiler_params=pltpu.CompilerParams(
            dimension_semantics=("arbitrary", "arbitrary"), vmem_limit_bytes=VMEM_LIMIT),
        name="in_proj_a",
    )(x2d, x2d, norm_w, w_t)


def _in_proj_b_kernel(xn_ref, wb_ref, wdt_ref, pb_ref, dt_ref):
    xn = xn_ref[...]
    pb_ref[...] = _dot_nt(xn, wb_ref[...]).astype(BF16)
    dt_ref[...] = _dot_nt(xn, wdt_ref[...])


def _in_proj_b(xn, w_b, w_dt, *, tm):
    rows = xn.shape[0]
    return pl.pallas_call(
        _in_proj_b_kernel,
        grid=(rows // tm,),
        in_specs=[
            pl.BlockSpec((tm, D_MODEL), lambda i: (i, 0)),
            pl.BlockSpec((D_PROJ_B, D_MODEL), lambda i: (0, 0)),
            pl.BlockSpec((DT_PAD, D_MODEL), lambda i: (0, 0)),
        ],
        out_specs=[
            pl.BlockSpec((tm, D_PROJ_B), lambda i: (i, 0)),
            pl.BlockSpec((tm, DT_PAD), lambda i: (i, 0)),
        ],
        out_shape=[
            jax.ShapeDtypeStruct((rows, D_PROJ_B), BF16),
            jax.ShapeDtypeStruct((rows, DT_PAD), F32),
        ],
        compiler_params=pltpu.CompilerParams(
            dimension_semantics=("arbitrary",), vmem_limit_bytes=VMEM_LIMIT),
        name="in_proj_b",
    )(xn, w_b, w_dt)


def _conv_tile(t, xs_ref, bc_ref, row0, xbuf_ref, convw_ref, convb_ref, xact_ref, bact_ref, cact_ref):
    rows = pl.ds(row0, CHUNK)
    cols = slice(t * LANES, (t + 1) * LANES)
    if t < D_SSD // LANES:
        raw = xs_ref[rows, cols]
    else:
        raw = bc_ref[rows, t * LANES - D_SSD:(t + 1) * LANES - D_SSD]
    xbuf_ref[t, 8:8 + CHUNK, :] = raw.astype(F32)
    acc = convb_ref[:, cols] + convw_ref[0:1, cols] * xbuf_ref[t, 5:5 + CHUNK, :]
    for i in range(1, CONV_WIDTH):
        acc = acc + convw_ref[i:i + 1, cols] * xbuf_ref[t, 5 + i:5 + i + CHUNK, :]
    act = _silu(acc)
    if t < D_SSD // LANES:
        xact_ref[:, cols] = act
    elif t < (D_SSD + D_BC // 2) // LANES:
        bact_ref[:, t * LANES - D_SSD:(t + 1) * LANES - D_SSD] = act.astype(BF16)
    else:
        o = t * LANES - D_SSD - D_BC // 2
        cact_ref[:, o:o + LANES] = act.astype(BF16)
    xbuf_ref[t, 0:8, :] = xbuf_ref[t, CHUNK:CHUNK + 8, :]


def _dt_cumsum(dt_ref, row0, dtb_ref, alog_ref, valid):
    dtr = dt_ref[pl.ds(row0, CHUNK), :]
    xx = dtr + dtb_ref[...]
    dt = jnp.maximum(xx, 0.0) + jnp.log1p(jnp.exp(-jnp.abs(xx)))
    if valid is not None:
        dt = dt * valid
    dt = jnp.where(_lane_iota((CHUNK, LANES)) < SSD_HEADS, dt, 0.0)
    dta = dt * (-jnp.exp(alog_ref[...]) * LOG2E)
    rr = _row_iota((CHUNK, 3 * CHUNK))
    cc = _lane_iota((CHUNK, 3 * CHUNK)) & (CHUNK - 1)
    tril3 = jnp.where(rr >= cc, 1.0, 0.0).astype(BF16)
    acs = _dot(tril3, jnp.concatenate(_split_bf16(dta, 3), axis=0))
    return dt, acs


def _expand_heads(dt, acs, e_ref):
    a_hi, a_mid, a_lo = [p.astype(F32) for p in _split_bf16(acs, 3)]
    a_stack = (a_hi + pltpu.roll(a_mid, 32, 1) + pltpu.roll(a_lo, 64, 1)).astype(BF16)
    acs_x = _dot(a_stack, e_ref[...])
    d_hi, d_mid = [p.astype(F32) for p in _split_bf16(dt, 2)]
    d_stack = (d_hi + pltpu.roll(d_mid, 32, 1)).astype(BF16)
    dt_x = _dot(d_stack, e_ref[...])
    return dt_x, acs_x


def _state_update(g, xact_ref, bact_ref, dt_x, acs_x, state_ref):
    gl = slice(g * GROUP_W, (g + 1) * GROUP_W)
    acs_g = acs_x[:, gl]
    a_last = acs_g[CHUNK - 1:CHUNK, :]
    x_g = xact_ref[:, gl]
    x_dt = x_g * dt_x[:, gl]
    xw = (x_dt * jnp.exp2(a_last - acs_g)).astype(BF16)
    b_g = bact_ref[:, g * D_STATE:(g + 1) * D_STATE]
    state_ref[g] = state_ref[g] * jnp.exp2(a_last) + _dot_tn(b_g, xw)
    return x_g, x_dt


def _rope_kv(k_ref, v_ref, row0, ck_ref, sk_ref, kk_ref, vv_ref, slot):
    rows = pl.ds(row0, CHUNK)
    lane = _lane_iota((CHUNK, LANES))
    first_of_pair = (lane & 63) < 32
    low_half = lane < CHUNK
    ck = ck_ref[rows, :]
    sk = sk_ref[rows, :]
    dst = slice(slot * CHUNK, (slot + 1) * CHUNK)
    for c in range(2):
        kx = k_ref[rows, c * LANES:(c + 1) * LANES].astype(F32)
        kr = kx * ck + pltpu.roll(kx, 64, 1) * sk
        kk_ref[2 * c, dst, :] = jnp.where(first_of_pair, kr, pltpu.roll(kr, 32, 1)).astype(BF16)
        kk_ref[2 * c + 1, dst, :] = jnp.where(first_of_pair, pltpu.roll(kr, 96, 1), kr).astype(BF16)
        vx = v_ref[rows, c * LANES:(c + 1) * LANES].astype(F32)
        vr = pltpu.roll(vx, 64, 1)
        vv_ref[2 * c, dst, 0:LANES] = jnp.where(low_half, vx, vr).astype(BF16)
        vv_ref[2 * c + 1, dst, 0:LANES] = jnp.where(low_half, vr, vx).astype(BF16)


def _chunk(refs, row0, chunk_idx, act, conv_next):
    (z_ref, q_ref, g_ref, k_ref, v_ref, dt_ref, cq_ref, sq_ref, ck_ref, sk_ref,
     dtb_ref, alog_ref, dskip_ref, nw_ref, e_ref, sinks_ref, out_ref,
     state_ref, xbd_ref, kk_ref, vv_ref) = refs
    xact_ref, bact_ref, cact_ref = act
    rows = pl.ds(row0, CHUNK)
    lane = _lane_iota((CHUNK, LANES))
    row = _row_iota((CHUNK, LANES))
    first_of_pair = (lane & 63) < 32
    low_half = lane < CHUNK
    s_idx = lane & (CHUNK - 1)
    diag = row == s_idx
    causal = row >= s_idx

    dt, acs = _dt_cumsum(dt_ref, row0, dtb_ref, alog_ref, None)

    kk_ref[:, 0:2 * CHUNK, :] = kk_ref[:, CHUNK:BAND, :]
    vv_ref[:, 0:2 * CHUNK, 0:LANES] = vv_ref[:, CHUNK:BAND, 0:LANES]
    _rope_kv(k_ref, v_ref, row0, ck_ref, sk_ref, kk_ref, vv_ref, WINDOW_CHUNKS)
    key_abs = (chunk_idx - WINDOW_CHUNKS) * CHUNK + _lane_iota((1, BAND))
    bias = jnp.where(key_abs >= PAD_LEAD, 0.0, NEG_BIG)
    cq = cq_ref[rows, :]
    sq = sq_ref[rows, :]
    scores = []
    for kv in range(ATT_KV_HEADS):
        q_rows = []
        for c in (2 * kv, 2 * kv + 1):
            qx = q_ref[rows, c * LANES:(c + 1) * LANES].astype(F32)
            qr = qx * cq + pltpu.roll(qx, 64, 1) * sq
            q_rows.append(jnp.where(first_of_pair, qr, 0.0).astype(BF16))
            q_rows.append(jnp.where(first_of_pair, 0.0, qr).astype(BF16))
        q_stack = jnp.concatenate(q_rows, axis=0)
        scores.append(_dot_nt(q_stack, kk_ref[kv]) + bias)

    cb2, y_off_raw = [], []
    for g in range(SSD_GROUPS):
        nl = slice(g * D_STATE, (g + 1) * D_STATE)
        b_g = bact_ref[:, nl]
        c_g = cact_ref[:, nl]
        cb2.append(_dot_nt(c_g, jnp.concatenate([b_g, b_g], axis=0)))
        y_off_raw.append(_dot(c_g, state_ref[g].astype(BF16)))
    conv_next(0)

    dt_x, acs_x = _expand_heads(dt, acs, e_ref)

    pvs, e_sinks = [], []
    for kv in range(ATT_KV_HEADS):
        s = scores[kv]
        p_rows, e_sink = [], []
        for r in range(ATT_REP):
            s_r = s[r * CHUNK:(r + 1) * CHUNK, :]
            sink = sinks_ref[kv * ATT_REP + r] * LOG2E
            m = jnp.maximum(jnp.max(s_r, axis=-1, keepdims=True), sink)
            p_rows.append(jnp.exp2(s_r - m).astype(BF16))
            e_sink.append(jnp.exp2(sink - m))
        pvs.append(_dot(jnp.concatenate(p_rows, axis=0), vv_ref[kv]))
        e_sinks.append(e_sink)
    conv_next(1)

    xs_g, y_diag = [], []
    for g in range(SSD_GROUPS):
        x_g, x_dt = _state_update(g, xact_ref, bact_ref, dt_x, acs_x, state_ref)
        xs_g.append(x_g)
        parts = []
        for pp in range(2):
            pls = slice(g * GROUP_W + pp * LANES, g * GROUP_W + (pp + 1) * LANES)
            col = acs_x[:, pls]
            rowv = jnp.sum(jnp.where(diag, col, 0.0), axis=0, keepdims=True)
            seg = jnp.where(causal, col - rowv, NEG_BIG)
            lmat = (jnp.exp2(seg) * cb2[g]).astype(BF16)
            xp = x_dt[:, pp * LANES:(pp + 1) * LANES].astype(BF16)
            p = 2 * g + pp
            xbd_ref[p, 0:CHUNK, 0:CHUNK] = xp[:, 0:CHUNK]
            xbd_ref[p, CHUNK:, CHUNK:] = xp[:, CHUNK:]
            parts.append(_dot(lmat, xbd_ref[p]))
        y_diag.append(parts)
    conv_next(2)

    for kv in range(ATT_KV_HEADS):
        pv = pvs[kv]
        for pr in range(2):
            r0, r1 = 2 * pr, 2 * pr + 1
            blk0 = pv[r0 * CHUNK:(r0 + 1) * CHUNK, :]
            blk1 = pv[r1 * CHUNK:(r1 + 1) * CHUNK, :]
            num = jnp.where(low_half, blk0[:, 0:LANES], blk1[:, 0:LANES])
            den = jnp.where(low_half, blk0[:, LANES:] + e_sinks[kv][r0], blk1[:, LANES:] + e_sinks[kv][r1])
            col = (2 * kv + pr) * LANES
            gate = _silu(g_ref[rows, col:col + LANES].astype(F32))
            out_ref[rows, D_SSD + col:D_SSD + col + LANES] = (num / den * gate).astype(BF16)
    conv_next(3)

    for g in range(SSD_GROUPS):
        gl = slice(g * GROUP_W, (g + 1) * GROUP_W)
        y = (jnp.concatenate(y_diag[g], axis=1) + y_off_raw[g] * jnp.exp2(acs_x[:, gl])
             + dskip_ref[:, gl] * xs_g[g])
        yz = y * _silu(z_ref[rows, gl].astype(F32))
        ms = jnp.mean(yz * yz, axis=-1, keepdims=True)
        out_ref[rows, gl] = (yz * lax.rsqrt(ms + EPS) * nw_ref[:, gl]).astype(BF16)


def _meta_kernel(xs_ref, bc_ref, k_ref, v_ref, dt_ref, ck_ref, sk_ref,
                 convw_ref, convb_ref, dtb_ref, alog_ref, e_ref,
                 state_ref, tail_ref, kk_ref, vv_ref,
                 xbuf_ref, xact_ref, bact_ref, cact_ref):
    xbuf_ref[:, 0:8, :] = jnp.zeros((CONV_TILES, 8, LANES), F32)
    state_ref[...] = jnp.zeros(state_ref.shape, F32)
    for t in range(CONV_TILES):
        _conv_tile(t, xs_ref, bc_ref, 0, xbuf_ref, convw_ref, convb_ref, xact_ref, bact_ref, cact_ref)
    tail_ref[...] = xbuf_ref[:, 0:8, :]
    valid = jnp.where(_row_iota((CHUNK, LANES)) >= PAD_LEAD, 1.0, 0.0)
    dt, acs = _dt_cumsum(dt_ref, 0, dtb_ref, alog_ref, valid)
    dt_x, acs_x = _expand_heads(dt, acs, e_ref)
    for g in range(SSD_GROUPS):
        _state_update(g, xact_ref, bact_ref, dt_x, acs_x, state_ref)
    _rope_kv(k_ref, v_ref, 0, ck_ref, sk_ref, kk_ref, vv_ref, 0)


def _mixer_kernel(z_ref, xs_ref, bc_ref, q_ref, g_ref, k_ref, v_ref, dt_ref,
                  cq_ref, sq_ref, ck_ref, sk_ref,
                  state0_ref, tail0_ref, kk0_ref, vv0_ref,
                  convw_ref, convb_ref, dtb_ref, alog_ref, dskip_ref, nw_ref, e_ref, sinks_ref,
                  out_ref,
                  state_ref, xbuf_ref, xact_ref, bact_ref, cact_ref, xbd_ref, kk_ref, vv_ref, *, chunks):
    j = pl.program_id(1)

    @pl.when(j == 0)
    def _():
        state_ref[...] = state0_ref[...]
        xbuf_ref[:, 0:8, :] = tail0_ref[...]
        xbd_ref[...] = jnp.zeros(xbd_ref.shape, BF16)
        kk_ref[...] = jnp.zeros(kk_ref.shape, BF16)
        vv_ref[:, :, 0:LANES] = jnp.zeros((ATT_KV_HEADS, BAND, LANES), BF16)
        vv_ref[:, :, LANES:] = jnp.ones((ATT_KV_HEADS, BAND, LANES), BF16)
        kk_ref[:, 2 * CHUNK:BAND, :] = kk0_ref[...]
        vv_ref[:, 2 * CHUNK:BAND, 0:LANES] = vv0_ref[...]

    refs = (z_ref, q_ref, g_ref, k_ref, v_ref, dt_ref, cq_ref, sq_ref, ck_ref, sk_ref,
            dtb_ref, alog_ref, dskip_ref, nw_ref, e_ref, sinks_ref, out_ref,
            state_ref, xbd_ref, kk_ref, vv_ref)
    act = lambda s: (xact_ref.at[s], bact_ref.at[s], cact_ref.at[s])

    def conv_tiles(tiles, row0, slot):
        for t in tiles:
            _conv_tile(t, xs_ref, bc_ref, row0, xbuf_ref, convw_ref, convb_ref, *act(slot))

    conv_tiles(range(CONV_TILES), 0, 0)
    quarter = CONV_TILES // 4

    def body(ci, carry):
        row0 = pl.multiple_of(ci * CHUNK, CHUNK)
        cur = ci % 2
        next_row0 = pl.multiple_of(jnp.minimum(ci + 1, chunks - 1) * CHUNK, CHUNK)
        conv_next = lambda k: conv_tiles(range(k * quarter, (k + 1) * quarter), next_row0, 1 - cur)
        _chunk(refs, row0, j * chunks + ci + 1, act(cur), conv_next)
        return carry

    lax.fori_loop(0, chunks, body, 0)


def _const_spec(shape):
    nd = len(shape)
    return pl.BlockSpec(shape, lambda *_: (0,) * nd)


def _meta_state(projm_a, projm_b, dtm, ckm, skm, conv_w, conv_b, dtb, alog, e_mat):
    col = lambda w, idx: pl.BlockSpec((CHUNK, w), lambda i: (0, idx))
    return pl.pallas_call(
        _meta_kernel,
        grid=(1,),
        in_specs=[
            col(D_SSD, 1), col(D_BC, 2), col(D_KV, 2 * D_ATT // D_KV), col(D_KV, 2 * D_ATT // D_KV + 1),
            _const_spec((CHUNK, DT_PAD)), _const_spec((CHUNK, LANES)), _const_spec((CHUNK, LANES)),
            _const_spec((CONV_WIDTH, D_CONV)), _const_spec((1, D_CONV)),
            _const_spec((1, DT_PAD)), _const_spec((1, DT_PAD)), _const_spec((LANES, D_SSD)),
        ],
        out_specs=[
            _const_spec((SSD_GROUPS, D_STATE, GROUP_W)), _const_spec((CONV_TILES, 8, LANES)),
            _const_spec((ATT_KV_HEADS, CHUNK, LANES)), _const_spec((ATT_KV_HEADS, CHUNK, LANES)),
        ],
        out_shape=[
            jax.ShapeDtypeStruct((SSD_GROUPS, D_STATE, GROUP_W), F32),
            jax.ShapeDtypeStruct((CONV_TILES, 8, LANES), F32),
            jax.ShapeDtypeStruct((ATT_KV_HEADS, CHUNK, LANES), BF16),
            jax.ShapeDtypeStruct((ATT_KV_HEADS, CHUNK, LANES), BF16),
        ],
        scratch_shapes=[
            pltpu.VMEM((CONV_TILES, CHUNK + 8, LANES), F32),
            pltpu.VMEM((CHUNK, D_SSD), F32),
            pltpu.VMEM((CHUNK, D_BC // 2), BF16),
            pltpu.VMEM((CHUNK, D_BC // 2), BF16),
        ],
        compiler_params=pltpu.CompilerParams(
            dimension_semantics=("arbitrary",), vmem_limit_bytes=VMEM_LIMIT),
        name="meta_state",
    )(projm_a, projm_a, projm_b, projm_b, dtm, ckm, skm, conv_w, conv_b, dtb, alog, e_mat)


def _mixer(proj_a, proj_b, dtr, tables, meta_state, params, *, batch, seq, tb):
    cq, sq, ck, sk = tables
    state0, tail0, kk0, vv0 = meta_state
    conv_w, conv_b, dtb, alog, dskip_x, ssd_nw, e_mat, sinks = params
    nblk = seq // tb
    rows = batch * seq
    chunks = tb // CHUNK
    col = lambda w, idx: pl.BlockSpec((tb, w), lambda b, j: (b * nblk + j, idx))
    tab = pl.BlockSpec((tb, LANES), lambda b, j: (j, 0))
    return pl.pallas_call(
        functools.partial(_mixer_kernel, chunks=chunks),
        grid=(batch, nblk),
        in_specs=[
            col(D_SSD, 0), col(D_SSD, 1), col(D_BC, 2), col(D_ATT, 0), col(D_ATT, 1),
            col(D_KV, 2 * D_ATT // D_KV), col(D_KV, 2 * D_ATT // D_KV + 1), col(DT_PAD, 0),
            tab, tab, tab, tab,
            _const_spec((SSD_GROUPS, D_STATE, GROUP_W)), _const_spec((CONV_TILES, 8, LANES)),
            _const_spec((ATT_KV_HEADS, CHUNK, LANES)), _const_spec((ATT_KV_HEADS, CHUNK, LANES)),
            _const_spec((CONV_WIDTH, D_CONV)), _const_spec((1, D_CONV)),
            _const_spec((1, DT_PAD)), _const_spec((1, DT_PAD)),
            _const_spec((1, D_SSD)), _const_spec((1, D_SSD)), _const_spec((LANES, D_SSD)),
            pl.BlockSpec(memory_space=pltpu.SMEM),
        ],
        out_specs=pl.BlockSpec((tb, D_MIX), lambda b, j: (b * nblk + j, 0)),
        out_shape=jax.ShapeDtypeStruct((rows, D_MIX), BF16),
        scratch_shapes=[
            pltpu.VMEM((SSD_GROUPS, D_STATE, GROUP_W), F32),
            pltpu.VMEM((CONV_TILES, CHUNK + 8, LANES), F32),
            pltpu.VMEM((2, CHUNK, D_SSD), F32),
            pltpu.VMEM((2, CHUNK, D_BC // 2), BF16),
            pltpu.VMEM((2, CHUNK, D_BC // 2), BF16),
            pltpu.VMEM((SSD_HEADS // 2, 2 * CHUNK, LANES), BF16),
            pltpu.VMEM((ATT_KV_HEADS, BAND, LANES), BF16),
            pltpu.VMEM((ATT_KV_HEADS, BAND, 2 * LANES), BF16),
        ],
        compiler_params=pltpu.CompilerParams(
            dimension_semantics=("arbitrary", "arbitrary"), vmem_limit_bytes=VMEM_LIMIT),
        name="mixer",
    )(proj_a, proj_a, proj_a, proj_b, proj_b, proj_b, proj_b, dtr, cq, sq, ck, sk,
      state0, tail0, kk0, vv0, conv_w, conv_b, dtb, alog, dskip_x, ssd_nw, e_mat, sinks)


def _out_proj_kernel(mix_ref, w_ref, x_ref, nw_ref, o_ref):
    o = _dot(mix_ref[...], w_ref[...])
    ms = jnp.mean(o * o, axis=-1, keepdims=True)
    o_ref[...] = x_ref[...] + o * lax.rsqrt(ms + EPS) * nw_ref[...]


def _out_proj(mix, w_out, x2d, norm_w, *, tm):
    rows = mix.shape[0]
    return pl.pallas_call(
        _out_proj_kernel,
        grid=(rows // tm,),
        in_specs=[
            pl.BlockSpec((tm, D_MIX), lambda i: (i, 0)),
            pl.BlockSpec((D_MIX, D_MODEL), lambda i: (0, 0)),
            pl.BlockSpec((tm, D_MODEL), lambda i: (i, 0)),
            pl.BlockSpec((1, D_MODEL), lambda i: (0, 0)),
        ],
        out_specs=pl.BlockSpec((tm, D_MODEL), lambda i: (i, 0)),
        out_shape=jax.ShapeDtypeStruct((rows, D_MODEL), F32),
        compiler_params=pltpu.CompilerParams(
            dimension_semantics=("arbitrary",), vmem_limit_bytes=VMEM_LIMIT),
        name="out_proj",
    )(mix, w_out, x2d, norm_w)


def _pair_layout(w_rows, heads):
    half = ATT_HEAD_DIM // 2
    w = w_rows.reshape(heads // 2, 2, 2, half, w_rows.shape[1])
    return w.transpose(0, 2, 1, 3, 4).reshape(heads * ATT_HEAD_DIM, w_rows.shape[1])


def _rope_tables(n_pos):
    half = ATT_HEAD_DIM // 2
    pos = jnp.arange(n_pos, dtype=jnp.int32) - PAD_LEAD
    inv = ROPE_THETA ** (-jnp.arange(half, dtype=F32) / half)
    ang = pos.astype(F32)[:, None] * inv[None, :]
    cos, sin = jnp.cos(ang), jnp.sin(ang)
    cos4 = jnp.concatenate([cos, cos, cos, cos], axis=1)
    sin4 = jnp.concatenate([-sin, -sin, sin, sin], axis=1)
    return cos4, sin4


def kernel(x, meta_tokens, norm_pre_w, w_in, conv_w, conv_b, dt_bias, a_log, d_skip, ssd_norm_w,
           attn_sinks, w_out, norm_post_w):
    batch, seq, _ = x.shape
    assert norm_pre_w.shape[0] == 1 and seq % CHUNK == 0
    rows = batch * seq
    x2d = x.reshape(rows, D_MODEL)

    w_t = jnp.swapaxes(w_in[0], 0, 1).astype(BF16)
    o = D_PROJ_A
    seg = {}
    for name, width in (("dt", SSD_HEADS), ("q", D_ATT), ("k", D_KV), ("v", D_KV), ("g", D_ATT)):
        seg[name] = w_t[o:o + width]
        o += width
    w_b = jnp.concatenate([_pair_layout(seg["q"], ATT_Q_HEADS), seg["g"],
                           _pair_layout(seg["k"], ATT_KV_HEADS), seg["v"]], axis=0)
    w_dt = jnp.pad(seg["dt"], ((0, DT_PAD - SSD_HEADS), (0, 0)))
    w_o = w_out[0].astype(BF16)

    pad_h = lambda v: jnp.pad(v.reshape(1, SSD_HEADS), ((0, 0), (0, DT_PAD - SSD_HEADS)))
    dtb = pad_h(dt_bias[0])
    alog = pad_h(a_log[0])
    dskip_x = jnp.repeat(d_skip[0], SSD_HEAD_DIM).reshape(1, D_SSD)
    e_rows = jnp.arange(LANES)
    e_mat = ((e_rows[:, None] % SSD_HEADS == jnp.arange(D_SSD)[None, :] // SSD_HEAD_DIM)
             & (e_rows[:, None] < 3 * SSD_HEADS)).astype(BF16)

    cos4, sin4 = _rope_tables(CHUNK + seq)
    scale = ATT_HEAD_DIM ** -0.5 * LOG2E
    tables = (cos4[CHUNK:] * scale, sin4[CHUNK:] * scale, cos4[CHUNK:], sin4[CHUNK:])

    npw = norm_pre_w[0].reshape(1, D_MODEL)
    cw, cb = conv_w[0], conv_b[0].reshape(1, D_CONV)

    xm = jnp.concatenate([jnp.zeros((PAD_LEAD, D_MODEL), x.dtype), meta_tokens.astype(x.dtype)], axis=0)
    projm_a, xnm = _in_proj_a(xm, npw, w_t, tm=CHUNK, tn=2048)
    projm_b, dtm = _in_proj_b(xnm, w_b, w_dt, tm=CHUNK)
    meta_state = _meta_state(projm_a, projm_b, dtm, cos4[:CHUNK], sin4[:CHUNK], cw, cb, dtb, alog, e_mat)

    proj_a, xn = _in_proj_a(x2d, npw, w_t, tm=512, tn=2048)
    proj_b, dtr = _in_proj_b(xn, w_b, w_dt, tm=1024)
    params = (cw, cb, dtb, alog, dskip_x, ssd_norm_w[0].reshape(1, D_SSD), e_mat, attn_sinks[0])
    mix = _mixer(proj_a, proj_b, dtr, tables, meta_state, params, batch=batch, seq=seq, tb=512)
    out = _out_proj(mix, w_o, x2d, norm_post_w[0].reshape(1, D_MODEL), tm=512)
    return out.reshape(batch, seq, D_MODEL)
```

```python
import functools

import jax
import jax.numpy as jnp
from jax import lax
from jax.experimental import pallas as pl
from jax.experimental.pallas import tpu as pltpu

D_MODEL = 2048
CHUNK = 64
N_META = 16
PAD_LEAD = CHUNK - N_META
EPS = 1e-6

SSD_HEADS = 32
SSD_HEAD_DIM = 64
D_SSD = SSD_HEADS * SSD_HEAD_DIM
SSD_GROUPS = 8
D_STATE = 128
CONV_WIDTH = 4
D_BC = 2 * SSD_GROUPS * D_STATE
D_CONV = D_SSD + D_BC
GROUP_W = D_SSD // SSD_GROUPS

ATT_Q_HEADS = 16
ATT_KV_HEADS = 4
ATT_REP = ATT_Q_HEADS // ATT_KV_HEADS
ATT_HEAD_DIM = 64
D_ATT = ATT_Q_HEADS * ATT_HEAD_DIM
D_KV = ATT_KV_HEADS * ATT_HEAD_DIM
WINDOW_CHUNKS = 2
BAND = (WINDOW_CHUNKS + 1) * CHUNK
ROPE_THETA = 10000.0
D_MIX = D_SSD + D_ATT

LANES = 128
DT_PAD = LANES
D_PROJ_A = D_SSD + D_SSD + D_BC
D_PROJ_B = D_ATT + D_ATT + D_KV + D_KV
CONV_TILES = D_CONV // LANES
NEG_BIG = -1e30
LOG2E = 1.4426950408889634
VMEM_LIMIT = 56 * 1024 * 1024

F32 = jnp.float32
BF16 = jnp.bfloat16


def _dot(a, b):
    return jnp.dot(a, b, preferred_element_type=F32)


def _dot_nt(a, b):
    return lax.dot_general(a, b, (((1,), (1,)), ((), ())), preferred_element_type=F32)


def _dot_tn(a, b):
    return lax.dot_general(a, b, (((0,), (0,)), ((), ())), preferred_element_type=F32)


def _silu(x):
    return x / (1.0 + jnp.exp2(x * (-LOG2E)))


def _split_bf16(x, parts):
    out = []
    r = x
    for _ in range(parts):
        p = r.astype(BF16)
        out.append(p)
        r = r - p.astype(F32)
    return out


def _lane_iota(shape):
    return lax.broadcasted_iota(jnp.int32, shape, 1)


def _row_iota(shape):
    return lax.broadcasted_iota(jnp.int32, shape, 0)


A_TILE = D_SSD
A_LANE_TILES = A_TILE // LANES


def _in_proj_a_kernel(x0_ref, xnext_ref, nw_ref, w_ref, convw_ref, convb_ref, tail0_ref,
                      pa_ref, xn_out_ref, tail_ref, xn_ref, cbuf_ref, *, strip, tiles_per_seq):
    i = pl.program_id(0)
    j = pl.program_id(1)
    tm = xnext_ref.shape[0]

    def norm_rows(src_ref, r0, nrows, slot):
        xv = src_ref[pl.ds(r0, nrows), :]
        ms = jnp.mean(xv * xv, axis=-1, keepdims=True)
        xn_ref[slot, pl.ds(r0, nrows), :] = (xv * lax.rsqrt(ms + EPS) * nw_ref[...]).astype(BF16)

    @pl.when((i == 0) & (j == 0))
    def _():
        def body(s, carry):
            norm_rows(x0_ref, pl.multiple_of(s * CHUNK, CHUNK), CHUNK, 0)
            return carry
        lax.fori_loop(0, tm // CHUNK, body, 0)

    @pl.when((i % tiles_per_seq == 0) & (j > 0))
    def _():
        tail_ref[j - 1] = tail0_ref[...]

    cur = i % 2

    def norm_ahead():
        r0 = pl.multiple_of(jnp.minimum(j * strip, tm - strip), CHUNK)
        norm_rows(xnext_ref, r0, strip, 1 - cur)

    @pl.when(j == 0)
    def _():
        xn_cur = xn_ref[cur]
        xn_out_ref[...] = xn_cur
        pa_ref[...] = _silu(_dot_nt(xn_cur, w_ref[...])).astype(BF16)
        norm_ahead()

    @pl.when(j > 0)
    def _():
        acc = _dot_nt(xn_ref[cur], w_ref[...])
        for t in range(A_LANE_TILES):
            cols = slice(t * LANES, (t + 1) * LANES)
            cbuf_ref[t, 0:8, :] = tail_ref[j - 1, t]
            cbuf_ref[t, 8:8 + tm, :] = acc[:, cols]
            conv = convb_ref[:, cols] + convw_ref[0:1, cols] * cbuf_ref[t, 5:5 + tm, :]
            for tap in range(1, CONV_WIDTH):
                conv = conv + convw_ref[tap:tap + 1, cols] * cbuf_ref[t, 5 + tap:5 + tap + tm, :]
            pa_ref[:, cols] = _silu(conv).astype(BF16)
            tail_ref[j - 1, t] = cbuf_ref[t, tm:tm + 8, :]
        norm_ahead()


def _in_proj_a(x2d, norm_w, w_t, conv_w, conv_b, tail0, *, tm, seq):
    rows = x2d.shape[0]
    tn = A_TILE
    nrt, nct = rows // tm, D_PROJ_A // tn
    strip = min(tm, -(-tm // nct // CHUNK) * CHUNK)
    assert strip * nct >= tm and strip <= tm and tm % CHUNK == 0 and seq % tm == 0
    last = nrt - 1
    conv_col = lambda i, j: (0, jnp.maximum(j - 1, 0))
    return pl.pallas_call(
        functools.partial(_in_proj_a_kernel, strip=strip, tiles_per_seq=seq // tm),
        grid=(nrt, nct),
        in_specs=[
            pl.BlockSpec((tm, D_MODEL), lambda i, j: (0, 0)),
            pl.BlockSpec((tm, D_MODEL), lambda i, j: (jnp.minimum(i + 1, last), 0)),
            pl.BlockSpec((1, D_MODEL), lambda i, j: (0, 0)),
            pl.BlockSpec((tn, D_MODEL), lambda i, j: (j, 0)),
            pl.BlockSpec((CONV_WIDTH, tn), conv_col),
            pl.BlockSpec((1, tn), conv_col),
            pl.BlockSpec((A_LANE_TILES, 8, LANES), lambda i, j: (jnp.maximum(j - 1, 0), 0, 0)),
        ],
        out_specs=[
            pl.BlockSpec((tm, tn), lambda i, j: (i, j)),
            pl.BlockSpec((tm, D_MODEL), lambda i, j: (i, 0)),
            pl.BlockSpec((nct - 1, A_LANE_TILES, 8, LANES), lambda i, j: (0, 0, 0, 0)),
        ],
        out_shape=[
            jax.ShapeDtypeStruct((rows, D_PROJ_A), BF16),
            jax.ShapeDtypeStruct((rows, D_MODEL), BF16),
            jax.ShapeDtypeStruct((nct - 1, A_LANE_TILES, 8, LANES), F32),
        ],
        scratch_shapes=[pltpu.VMEM((2, tm, D_MODEL), BF16),
                        pltpu.VMEM((A_LANE_TILES, tm + 8, LANES), F32)],
        compiler_params=pltpu.CompilerParams(
            dimension_semantics=("arbitrary", "arbitrary"), vmem_limit_bytes=VMEM_LIMIT),
        name="in_proj_a",
    )(x2d, x2d, norm_w, w_t, conv_w, conv_b, tail0)


def _in_proj_b_kernel(xn_ref, wb_ref, wdt_ref, pb_ref, dt_ref):
    xn = xn_ref[...]
    pb_ref[...] = _dot_nt(xn, wb_ref[...]).astype(BF16)
    dt_ref[...] = _dot_nt(xn, wdt_ref[...])


def _in_proj_b(xn, w_b, w_dt, *, tm):
    rows = xn.shape[0]
    return pl.pallas_call(
        _in_proj_b_kernel,
        grid=(rows // tm,),
        in_specs=[
            pl.BlockSpec((tm, D_MODEL), lambda i: (i, 0)),
            pl.BlockSpec((D_PROJ_B, D_MODEL), lambda i: (0, 0)),
            pl.BlockSpec((DT_PAD, D_MODEL), lambda i: (0, 0)),
        ],
        out_specs=[
            pl.BlockSpec((tm, D_PROJ_B), lambda i: (i, 0)),
            pl.BlockSpec((tm, DT_PAD), lambda i: (i, 0)),
        ],
        out_shape=[
            jax.ShapeDtypeStruct((rows, D_PROJ_B), BF16),
            jax.ShapeDtypeStruct((rows, DT_PAD), F32),
        ],
        compiler_params=pltpu.CompilerParams(
            dimension_semantics=("arbitrary",), vmem_limit_bytes=VMEM_LIMIT),
        name="in_proj_b",
    )(xn, w_b, w_dt)


def _dt_cumsum(dt_ref, row0, dtb_ref, alog_ref, valid):
    dtr = dt_ref[pl.ds(row0, CHUNK), :]
    xx = dtr + dtb_ref[...]
    dt = jnp.maximum(xx, 0.0) + jnp.log1p(jnp.exp(-jnp.abs(xx)))
    if valid is not None:
        dt = dt * valid
    dt = jnp.where(_lane_iota((CHUNK, LANES)) < SSD_HEADS, dt, 0.0)
    dta = dt * (-jnp.exp(alog_ref[...]) * LOG2E)
    rr = _row_iota((CHUNK, 3 * CHUNK))
    cc = _lane_iota((CHUNK, 3 * CHUNK)) & (CHUNK - 1)
    tril3 = jnp.where(rr >= cc, 1.0, 0.0).astype(BF16)
    acs = _dot(tril3, jnp.concatenate(_split_bf16(dta, 3), axis=0))
    return dt, acs


def _expand_heads(dt, acs, e_ref):
    a_hi, a_mid, a_lo = [p.astype(F32) for p in _split_bf16(acs, 3)]
    a_stack = (a_hi + pltpu.roll(a_mid, 32, 1) + pltpu.roll(a_lo, 64, 1)).astype(BF16)
    acs_x = _dot(a_stack, e_ref[...])
    d_hi, d_mid = [p.astype(F32) for p in _split_bf16(dt, 2)]
    d_stack = (d_hi + pltpu.roll(d_mid, 32, 1)).astype(BF16)
    dt_x = _dot(d_stack, e_ref[...])
    return dt_x, acs_x


def _state_update(g, xs_ref, bc_ref, rows, dt_x, acs_x, state_ref):
    gl = slice(g * GROUP_W, (g + 1) * GROUP_W)
    acs_g = acs_x[:, gl]
    a_last = acs_g[CHUNK - 1:CHUNK, :]
    x_g = xs_ref[rows, gl].astype(F32)
    x_dt = x_g * dt_x[:, gl]
    xw = (x_dt * jnp.exp2(a_last - acs_g)).astype(BF16)
    b_g = bc_ref[rows, g * D_STATE:(g + 1) * D_STATE]
    state_ref[g] = state_ref[g] * jnp.exp2(a_last) + _dot_tn(b_g, xw)
    return x_g, x_dt


def _rope_kv(k_ref, v_ref, row0, ck_ref, sk_ref, kk_ref, vv_ref, slot):
    rows = pl.ds(row0, CHUNK)
    lane = _lane_iota((CHUNK, LANES))
    first_of_pair = (lane & 63) < 32
    low_half = lane < CHUNK
    ck = ck_ref[rows, :]
    sk = sk_ref[rows, :]
    dst = slice(slot * CHUNK, (slot + 1) * CHUNK)
    for c in range(2):
        kx = k_ref[rows, c * LANES:(c + 1) * LANES].astype(F32)
        kr = kx * ck + pltpu.roll(kx, 64, 1) * sk
        kk_ref[2 * c, dst, :] = jnp.where(first_of_pair, kr, pltpu.roll(kr, 32, 1)).astype(BF16)
        kk_ref[2 * c + 1, dst, :] = jnp.where(first_of_pair, pltpu.roll(kr, 96, 1), kr).astype(BF16)
        vx = v_ref[rows, c * LANES:(c + 1) * LANES].astype(F32)
        vr = pltpu.roll(vx, 64, 1)
        vv_ref[2 * c, dst, 0:LANES] = jnp.where(low_half, vx, vr).astype(BF16)
        vv_ref[2 * c + 1, dst, 0:LANES] = jnp.where(low_half, vr, vx).astype(BF16)


def _chunk(refs, row0, chunk_idx):
    (z_ref, xs_ref, bc_ref, q_ref, g_ref, k_ref, v_ref, dt_ref, cq_ref, sq_ref, ck_ref, sk_ref,
     dtb_ref, alog_ref, dskip_ref, nw_ref, e_ref, sinks_ref, out_ref,
     state_ref, xbd_ref, kk_ref, vv_ref) = refs
    rows = pl.ds(row0, CHUNK)
    lane = _lane_iota((CHUNK, LANES))
    row = _row_iota((CHUNK, LANES))
    first_of_pair = (lane & 63) < 32
    low_half = lane < CHUNK
    s_idx = lane & (CHUNK - 1)
    diag = row == s_idx
    causal = row >= s_idx

    dt, acs = _dt_cumsum(dt_ref, row0, dtb_ref, alog_ref, None)

    kk_ref[:, 0:2 * CHUNK, :] = kk_ref[:, CHUNK:BAND, :]
    vv_ref[:, 0:2 * CHUNK, 0:LANES] = vv_ref[:, CHUNK:BAND, 0:LANES]
    _rope_kv(k_ref, v_ref, row0, ck_ref, sk_ref, kk_ref, vv_ref, WINDOW_CHUNKS)
    key_abs = (chunk_idx - WINDOW_CHUNKS) * CHUNK + _lane_iota((1, BAND))
    bias = jnp.where(key_abs >= PAD_LEAD, 0.0, NEG_BIG)
    cq = cq_ref[rows, :]
    sq = sq_ref[rows, :]
    scores = []
    for kv in range(ATT_KV_HEADS):
        q_rows = []
        for c in (2 * kv, 2 * kv + 1):
            qx = q_ref[rows, c * LANES:(c + 1) * LANES].astype(F32)
            qr = qx * cq + pltpu.roll(qx, 64, 1) * sq
            q_rows.append(jnp.where(first_of_pair, qr, 0.0).astype(BF16))
            q_rows.append(jnp.where(first_of_pair, 0.0, qr).astype(BF16))
        q_stack = jnp.concatenate(q_rows, axis=0)
        scores.append(_dot_nt(q_stack, kk_ref[kv]) + bias)

    cb2, y_off_raw = [], []
    for g in range(SSD_GROUPS):
        nl = slice(g * D_STATE, (g + 1) * D_STATE)
        b_g = bc_ref[rows, nl]
        c_g = bc_ref[rows, D_BC // 2 + g * D_STATE:D_BC // 2 + (g + 1) * D_STATE]
        cb2.append(_dot_nt(c_g, jnp.concatenate([b_g, b_g], axis=0)))
        y_off_raw.append(_dot(c_g, state_ref[g].astype(BF16)))

    dt_x, acs_x = _expand_heads(dt, acs, e_ref)

    pvs, e_sinks = [], []
    for kv in range(ATT_KV_HEADS):
        s = scores[kv]
        p_rows, e_sink = [], []
        for r in range(ATT_REP):
            s_r = s[r * CHUNK:(r + 1) * CHUNK, :]
            sink = sinks_ref[kv * ATT_REP + r] * LOG2E
            m = jnp.maximum(jnp.max(s_r, axis=-1, keepdims=True), sink)
            p_rows.append(jnp.exp2(s_r - m).astype(BF16))
            e_sink.append(jnp.exp2(sink - m))
        pvs.append(_dot(jnp.concatenate(p_rows, axis=0), vv_ref[kv]))
        e_sinks.append(e_sink)

    xs_g, y_diag = [], []
    for g in range(SSD_GROUPS):
        x_g, x_dt = _state_update(g, xs_ref, bc_ref, rows, dt_x, acs_x, state_ref)
        xs_g.append(x_g)
        parts = []
        for pp in range(2):
            pls = slice(g * GROUP_W + pp * LANES, g * GROUP_W + (pp + 1) * LANES)
            col = acs_x[:, pls]
            rowv = jnp.sum(jnp.where(diag, col, 0.0), axis=0, keepdims=True)
            seg = jnp.where(causal, col - rowv, NEG_BIG)
            lmat = (jnp.exp2(seg) * cb2[g]).astype(BF16)
            xp = x_dt[:, pp * LANES:(pp + 1) * LANES].astype(BF16)
            p = 2 * g + pp
            xbd_ref[p, 0:CHUNK, 0:CHUNK] = xp[:, 0:CHUNK]
            xbd_ref[p, CHUNK:, CHUNK:] = xp[:, CHUNK:]
            parts.append(_dot(lmat, xbd_ref[p]))
        y_diag.append(parts)

    for kv in range(ATT_KV_HEADS):
        pv = pvs[kv]
        for pr in range(2):
            r0, r1 = 2 * pr, 2 * pr + 1
            blk0 = pv[r0 * CHUNK:(r0 + 1) * CHUNK, :]
            blk1 = pv[r1 * CHUNK:(r1 + 1) * CHUNK, :]
            num = jnp.where(low_half, blk0[:, 0:LANES], blk1[:, 0:LANES])
            den = jnp.where(low_half, blk0[:, LANES:] + e_sinks[kv][r0], blk1[:, LANES:] + e_sinks[kv][r1])
            col = (2 * kv + pr) * LANES
            gate = _silu(g_ref[rows, col:col + LANES].astype(F32))
            out_ref[rows, D_SSD + col:D_SSD + col + LANES] = (num / den * gate).astype(BF16)

    for g in range(SSD_GROUPS):
        gl = slice(g * GROUP_W, (g + 1) * GROUP_W)
        y = (jnp.concatenate(y_diag[g], axis=1) + y_off_raw[g] * jnp.exp2(acs_x[:, gl])
             + dskip_ref[:, gl] * xs_g[g])
        yz = y * z_ref[rows, gl].astype(F32)
        ms = jnp.mean(yz * yz, axis=-1, keepdims=True)
        out_ref[rows, gl] = (yz * lax.rsqrt(ms + EPS) * nw_ref[:, gl]).astype(BF16)


def _meta_kernel(xs_ref, bc_ref, k_ref, v_ref, dt_ref, ck_ref, sk_ref,
                 dtb_ref, alog_ref, e_ref,
                 state_ref, kk_ref, vv_ref):
    state_ref[...] = jnp.zeros(state_ref.shape, F32)
    valid = jnp.where(_row_iota((CHUNK, LANES)) >= PAD_LEAD, 1.0, 0.0)
    dt, acs = _dt_cumsum(dt_ref, 0, dtb_ref, alog_ref, valid)
    dt_x, acs_x = _expand_heads(dt, acs, e_ref)
    for g in range(SSD_GROUPS):
        _state_update(g, xs_ref, bc_ref, pl.ds(0, CHUNK), dt_x, acs_x, state_ref)
    _rope_kv(k_ref, v_ref, 0, ck_ref, sk_ref, kk_ref, vv_ref, 0)


def _mixer_kernel(z_ref, xs_ref, bc_ref, q_ref, g_ref, k_ref, v_ref, dt_ref,
                  cq_ref, sq_ref, ck_ref, sk_ref,
                  state0_ref, kk0_ref, vv0_ref,
                  dtb_ref, alog_ref, dskip_ref, nw_ref, e_ref, sinks_ref,
                  out_ref,
                  state_ref, xbd_ref, kk_ref, vv_ref, *, chunks):
    j = pl.program_id(1)

    @pl.when(j == 0)
    def _():
        state_ref[...] = state0_ref[...]
        xbd_ref[...] = jnp.zeros(xbd_ref.shape, BF16)
        kk_ref[...] = jnp.zeros(kk_ref.shape, BF16)
        vv_ref[:, :, 0:LANES] = jnp.zeros((ATT_KV_HEADS, BAND, LANES), BF16)
        vv_ref[:, :, LANES:] = jnp.ones((ATT_KV_HEADS, BAND, LANES), BF16)
        kk_ref[:, 2 * CHUNK:BAND, :] = kk0_ref[...]
        vv_ref[:, 2 * CHUNK:BAND, 0:LANES] = vv0_ref[...]

    refs = (z_ref, xs_ref, bc_ref, q_ref, g_ref, k_ref, v_ref, dt_ref, cq_ref, sq_ref, ck_ref, sk_ref,
            dtb_ref, alog_ref, dskip_ref, nw_ref, e_ref, sinks_ref, out_ref,
            state_ref, xbd_ref, kk_ref, vv_ref)

    def body(ci, carry):
        row0 = pl.multiple_of(ci * CHUNK, CHUNK)
        _chunk(refs, row0, j * chunks + ci + 1)
        return carry

    lax.fori_loop(0, chunks, body, 0)


def _const_spec(shape):
    nd = len(shape)
    return pl.BlockSpec(shape, lambda *_: (0,) * nd)


def _meta_state(projm_a, projm_b, dtm, ckm, skm, dtb, alog, e_mat):
    col = lambda w, idx: pl.BlockSpec((CHUNK, w), lambda i: (0, idx))
    return pl.pallas_call(
        _meta_kernel,
        grid=(1,),
        in_specs=[
            col(D_SSD, 1), col(D_BC, 2), col(D_KV, 2 * D_ATT // D_KV), col(D_KV, 2 * D_ATT // D_KV + 1),
            _const_spec((CHUNK, DT_PAD)), _const_spec((CHUNK, LANES)), _const_spec((CHUNK, LANES)),
            _const_spec((1, DT_PAD)), _const_spec((1, DT_PAD)), _const_spec((LANES, D_SSD)),
        ],
        out_specs=[
            _const_spec((SSD_GROUPS, D_STATE, GROUP_W)),
            _const_spec((ATT_KV_HEADS, CHUNK, LANES)), _const_spec((ATT_KV_HEADS, CHUNK, LANES)),
        ],
        out_shape=[
            jax.ShapeDtypeStruct((SSD_GROUPS, D_STATE, GROUP_W), F32),
            jax.ShapeDtypeStruct((ATT_KV_HEADS, CHUNK, LANES), BF16),
            jax.ShapeDtypeStruct((ATT_KV_HEADS, CHUNK, LANES), BF16),
        ],
        compiler_params=pltpu.CompilerParams(
            dimension_semantics=("arbitrary",), vmem_limit_bytes=VMEM_LIMIT),
        name="meta_state",
    )(projm_a, projm_a, projm_b, projm_b, dtm, ckm, skm, dtb, alog, e_mat)


def _mixer(proj_a, proj_b, dtr, tables, meta_state, params, *, batch, seq, tb):
    cq, sq, ck, sk = tables
    state0, kk0, vv0 = meta_state
    dtb, alog, dskip_x, ssd_nw, e_mat, sinks = params
    nblk = seq // tb
    rows = batch * seq
    chunks = tb // CHUNK
    col = lambda w, idx: pl.BlockSpec((tb, w), lambda b, j: (b * nblk + j, idx))
    tab = pl.BlockSpec((tb, LANES), lambda b, j: (j, 0))
    return pl.pallas_call(
        functools.partial(_mixer_kernel, chunks=chunks),
        grid=(batch, nblk),
        in_specs=[
            col(D_SSD, 0), col(D_SSD, 1), col(D_BC, 2), col(D_ATT, 0), col(D_ATT, 1),
            col(D_KV, 2 * D_ATT // D_KV), col(D_KV, 2 * D_ATT // D_KV + 1), col(DT_PAD, 0),
            tab, tab, tab, tab,
            _const_spec((SSD_GROUPS, D_STATE, GROUP_W)),
            _const_spec((ATT_KV_HEADS, CHUNK, LANES)), _const_spec((ATT_KV_HEADS, CHUNK, LANES)),
            _const_spec((1, DT_PAD)), _const_spec((1, DT_PAD)),
            _const_spec((1, D_SSD)), _const_spec((1, D_SSD)), _const_spec((LANES, D_SSD)),
            pl.BlockSpec(memory_space=pltpu.SMEM),
        ],
        out_specs=pl.BlockSpec((tb, D_MIX), lambda b, j: (b * nblk + j, 0)),
        out_shape=jax.ShapeDtypeStruct((rows, D_MIX), BF16),
        scratch_shapes=[
            pltpu.VMEM((SSD_GROUPS, D_STATE, GROUP_W), F32),
            pltpu.VMEM((SSD_HEADS // 2, 2 * CHUNK, LANES), BF16),
            pltpu.VMEM((ATT_KV_HEADS, BAND, LANES), BF16),
            pltpu.VMEM((ATT_KV_HEADS, BAND, 2 * LANES), BF16),
        ],
        compiler_params=pltpu.CompilerParams(
            dimension_semantics=("arbitrary", "arbitrary"), vmem_limit_bytes=VMEM_LIMIT),
        name="mixer",
    )(proj_a, proj_a, proj_a, proj_b, proj_b, proj_b, proj_b, dtr, cq, sq, ck, sk,
      state0, kk0, vv0, dtb, alog, dskip_x, ssd_nw, e_mat, sinks)


def _out_proj_kernel(mix_ref, w_ref, x_ref, nw_ref, o_ref):
    o = _dot(mix_ref[...], w_ref[...])
    ms = jnp.mean(o * o, axis=-1, keepdims=True)
    o_ref[...] = x_ref[...] + o * lax.rsqrt(ms + EPS) * nw_ref[...]


def _out_proj(mix, w_out, x2d, norm_w, *, tm):
    rows = mix.shape[0]
    return pl.pallas_call(
        _out_proj_kernel,
        grid=(rows // tm,),
        in_specs=[
            pl.BlockSpec((tm, D_MIX), lambda i: (i, 0)),
            pl.BlockSpec((D_MIX, D_MODEL), lambda i: (0, 0)),
            pl.BlockSpec((tm, D_MODEL), lambda i: (i, 0)),
            pl.BlockSpec((1, D_MODEL), lambda i: (0, 0)),
        ],
        out_specs=pl.BlockSpec((tm, D_MODEL), lambda i: (i, 0)),
        out_shape=jax.ShapeDtypeStruct((rows, D_MODEL), F32),
        compiler_params=pltpu.CompilerParams(
            dimension_semantics=("arbitrary",), vmem_limit_bytes=VMEM_LIMIT),
        name="out_proj",
    )(mix, w_out, x2d, norm_w)


def _pair_layout(w_rows, heads):
    half = ATT_HEAD_DIM // 2
    w = w_rows.reshape(heads // 2, 2, 2, half, w_rows.shape[1])
    return w.transpose(0, 2, 1, 3, 4).reshape(heads * ATT_HEAD_DIM, w_rows.shape[1])


def _rope_tables(n_pos):
    half = ATT_HEAD_DIM // 2
    pos = jnp.arange(n_pos, dtype=jnp.int32) - PAD_LEAD
    inv = ROPE_THETA ** (-jnp.arange(half, dtype=F32) / half)
    ang = pos.astype(F32)[:, None] * inv[None, :]
    cos, sin = jnp.cos(ang), jnp.sin(ang)
    cos4 = jnp.concatenate([cos, cos, cos, cos], axis=1)
    sin4 = jnp.concatenate([-sin, -sin, sin, sin], axis=1)
    return cos4, sin4


def kernel(x, meta_tokens, norm_pre_w, w_in, conv_w, conv_b, dt_bias, a_log, d_skip, ssd_norm_w,
           attn_sinks, w_out, norm_post_w):
    batch, seq, _ = x.shape
    assert norm_pre_w.shape[0] == 1 and seq % CHUNK == 0
    rows = batch * seq
    x2d = x.reshape(rows, D_MODEL)

    w_t = jnp.swapaxes(w_in[0], 0, 1).astype(BF16)
    o = D_PROJ_A
    seg = {}
    for name, width in (("dt", SSD_HEADS), ("q", D_ATT), ("k", D_KV), ("v", D_KV), ("g", D_ATT)):
        seg[name] = w_t[o:o + width]
        o += width
    w_b = jnp.concatenate([_pair_layout(seg["q"], ATT_Q_HEADS), seg["g"],
                           _pair_layout(seg["k"], ATT_KV_HEADS), seg["v"]], axis=0)
    w_dt = jnp.pad(seg["dt"], ((0, DT_PAD - SSD_HEADS), (0, 0)))
    w_o = w_out[0].astype(BF16)

    pad_h = lambda v: jnp.pad(v.reshape(1, SSD_HEADS), ((0, 0), (0, DT_PAD - SSD_HEADS)))
    dtb = pad_h(dt_bias[0])
    alog = pad_h(a_log[0])
    dskip_x = jnp.repeat(d_skip[0], SSD_HEAD_DIM).reshape(1, D_SSD)
    e_rows = jnp.arange(LANES)
    e_mat = ((e_rows[:, None] % SSD_HEADS == jnp.arange(D_SSD)[None, :] // SSD_HEAD_DIM)
             & (e_rows[:, None] < 3 * SSD_HEADS)).astype(BF16)

    cos4, sin4 = _rope_tables(CHUNK + seq)
    scale = ATT_HEAD_DIM ** -0.5 * LOG2E
    tables = (cos4[CHUNK:] * scale, sin4[CHUNK:] * scale, cos4[CHUNK:], sin4[CHUNK:])

    npw = norm_pre_w[0].reshape(1, D_MODEL)
    cw, cb = conv_w[0], conv_b[0].reshape(1, D_CONV)

    xm = jnp.concatenate([jnp.zeros((PAD_LEAD, D_MODEL), x.dtype), meta_tokens.astype(x.dtype)], axis=0)
    n_tail = (D_PROJ_A // A_TILE - 1) * A_LANE_TILES
    no_tail = jnp.zeros((n_tail, 8, LANES), F32)
    projm_a, xnm, tail_m = _in_proj_a(xm, npw, w_t, cw, cb, no_tail, tm=CHUNK, seq=CHUNK)
    projm_b, dtm = _in_proj_b(xnm, w_b, w_dt, tm=CHUNK)
    meta_state = _meta_state(projm_a, projm_b, dtm, cos4[:CHUNK], sin4[:CHUNK], dtb, alog, e_mat)

    proj_a, xn, _ = _in_proj_a(x2d, npw, w_t, cw, cb, tail_m.reshape(n_tail, 8, LANES), tm=512, seq=seq)
    proj_b, dtr = _in_proj_b(xn, w_b, w_dt, tm=1024)
    params = (dtb, alog, dskip_x, ssd_norm_w[0].reshape(1, D_SSD), e_mat, attn_sinks[0])
    mix = _mixer(proj_a, proj_b, dtr, tables, meta_state, params, batch=batch, seq=seq, tb=512)
    out = _out_proj(mix, w_o, x2d, norm_post_w[0].reshape(1, D_MODEL), tm=512)
    return out.reshape(batch, seq, D_MODEL)
```

```python
import functools

import jax
import jax.numpy as jnp
from jax import lax
from jax.experimental import pallas as pl
from jax.experimental.pallas import tpu as pltpu

D_MODEL = 2048
CHUNK = 64
N_META = 16
PAD_LEAD = CHUNK - N_META
EPS = 1e-6

SSD_HEADS = 32
SSD_HEAD_DIM = 64
D_SSD = SSD_HEADS * SSD_HEAD_DIM
SSD_GROUPS = 8
D_STATE = 128
CONV_WIDTH = 4
D_BC = 2 * SSD_GROUPS * D_STATE
D_CONV = D_SSD + D_BC
GROUP_W = D_SSD // SSD_GROUPS

ATT_Q_HEADS = 16
ATT_KV_HEADS = 4
ATT_REP = ATT_Q_HEADS // ATT_KV_HEADS
ATT_HEAD_DIM = 64
D_ATT = ATT_Q_HEADS * ATT_HEAD_DIM
D_KV = ATT_KV_HEADS * ATT_HEAD_DIM
WINDOW_CHUNKS = 2
BAND = (WINDOW_CHUNKS + 1) * CHUNK
ROPE_THETA = 10000.0
D_MIX = D_SSD + D_ATT

LANES = 128
DT_PAD = LANES
D_PROJ_A = D_SSD + D_SSD + D_BC
D_PROJ_B = D_ATT + D_ATT + D_KV + D_KV
NEG_BIG = -1e30
LOG2E = 1.4426950408889634
VMEM_LIMIT = 56 * 1024 * 1024
TM_A = 512
TM_B = 1024
TB_MIX = 512
TM_OUT = 512

F32 = jnp.float32
BF16 = jnp.bfloat16


def _dot(a, b):
    return jnp.dot(a, b, preferred_element_type=F32)


def _dot_nt(a, b):
    return lax.dot_general(a, b, (((1,), (1,)), ((), ())), preferred_element_type=F32)


def _dot_tn(a, b):
    return lax.dot_general(a, b, (((0,), (0,)), ((), ())), preferred_element_type=F32)


def _silu(x):
    return x / (1.0 + jnp.exp2(x * (-LOG2E)))


def _split_bf16(x, parts):
    out = []
    r = x
    for _ in range(parts):
        p = r.astype(BF16)
        out.append(p)
        r = r - p.astype(F32)
    return out


def _lane_iota(shape):
    return lax.broadcasted_iota(jnp.int32, shape, 1)


def _row_iota(shape):
    return lax.broadcasted_iota(jnp.int32, shape, 0)


A_TILE = D_SSD
A_LANE_TILES = A_TILE // LANES


def _in_proj_a_kernel(x0_ref, xnext_ref, nw_ref, w_ref, convw_ref, convb_ref, tail0_ref,
                      pa_ref, xn_out_ref, tail_ref, xn_cur_ref, xn_next_ref, cbuf_ref, *, strip, tiles_per_seq):
    i = pl.program_id(0)
    j = pl.program_id(1)
    tm = xnext_ref.shape[0]

    def norm_rows(src_ref, r0, nrows):
        xv = src_ref[pl.ds(r0, nrows), :]
        ms = jnp.mean(xv * xv, axis=-1, keepdims=True)
        xn_next_ref[pl.ds(r0, nrows), :] = (xv * lax.rsqrt(ms + EPS) * nw_ref[...]).astype(BF16)

    @pl.when((i == 0) & (j == 0))
    def _():
        def body(s, carry):
            norm_rows(x0_ref, pl.multiple_of(s * CHUNK, CHUNK), CHUNK)
            return carry
        lax.fori_loop(0, tm // CHUNK, body, 0)

    @pl.when((i % tiles_per_seq == 0) & (j > 0))
    def _():
        tail_ref[j - 1] = tail0_ref[...]

    def norm_ahead():
        r0 = pl.multiple_of(jnp.minimum(j * strip, tm - strip), CHUNK)
        norm_rows(xnext_ref, r0, strip)

    @pl.when(j == 0)
    def _():
        xn_cur = xn_next_ref[...]
        xn_cur_ref[...] = xn_cur
        xn_out_ref[...] = xn_cur
        norm_ahead()
        pa_ref[...] = _silu(_dot_nt(xn_cur, w_ref[...])).astype(BF16)

    @pl.when(j > 0)
    def _():
        norm_ahead()
        acc = _dot_nt(xn_cur_ref[...], w_ref[...])
        for t in range(A_LANE_TILES):
            cols = slice(t * LANES, (t + 1) * LANES)
            cbuf_ref[t, 0:8, :] = tail_ref[j - 1, t]
            cbuf_ref[t, 8:8 + tm, :] = acc[:, cols]
            conv = convb_ref[:, cols] + convw_ref[0:1, cols] * cbuf_ref[t, 5:5 + tm, :]
            for tap in range(1, CONV_WIDTH):
                conv = conv + convw_ref[tap:tap + 1, cols] * cbuf_ref[t, 5 + tap:5 + tap + tm, :]
            pa_ref[:, cols] = _silu(conv).astype(BF16)
            tail_ref[j - 1, t] = cbuf_ref[t, tm:tm + 8, :]


def _in_proj_a(x2d, norm_w, w_t, conv_w, conv_b, tail0, *, tm, seq):
    rows = x2d.shape[0]
    tn = A_TILE
    nrt, nct = rows // tm, D_PROJ_A // tn
    strip = min(tm, -(-tm // nct // CHUNK) * CHUNK)
    assert strip * nct >= tm and strip <= tm and tm % CHUNK == 0 and seq % tm == 0
    last = nrt - 1
    conv_col = lambda i, j: (0, jnp.maximum(j - 1, 0))
    return pl.pallas_call(
        functools.partial(_in_proj_a_kernel, strip=strip, tiles_per_seq=seq // tm),
        grid=(nrt, nct),
        in_specs=[
            pl.BlockSpec((tm, D_MODEL), lambda i, j: (0, 0)),
            pl.BlockSpec((tm, D_MODEL), lambda i, j: (jnp.minimum(i + 1, last), 0)),
            pl.BlockSpec((1, D_MODEL), lambda i, j: (0, 0)),
            pl.BlockSpec((tn, D_MODEL), lambda i, j: (j, 0)),
            pl.BlockSpec((CONV_WIDTH, tn), conv_col),
            pl.BlockSpec((1, tn), conv_col),
            pl.BlockSpec((A_LANE_TILES, 8, LANES), lambda i, j: (jnp.maximum(j - 1, 0), 0, 0)),
        ],
        out_specs=[
            pl.BlockSpec((tm, tn), lambda i, j: (i, j)),
            pl.BlockSpec((tm, D_MODEL), lambda i, j: (i, 0)),
            pl.BlockSpec((nct - 1, A_LANE_TILES, 8, LANES), lambda i, j: (0, 0, 0, 0)),
        ],
        out_shape=[
            jax.ShapeDtypeStruct((rows, D_PROJ_A), BF16),
            jax.ShapeDtypeStruct((rows, D_MODEL), BF16),
            jax.ShapeDtypeStruct((nct - 1, A_LANE_TILES, 8, LANES), F32),
        ],
        scratch_shapes=[pltpu.VMEM((tm, D_MODEL), BF16),
                        pltpu.VMEM((tm, D_MODEL), BF16),
                        pltpu.VMEM((A_LANE_TILES, tm + 8, LANES), F32)],
        compiler_params=pltpu.CompilerParams(
            dimension_semantics=("arbitrary", "arbitrary"), vmem_limit_bytes=VMEM_LIMIT),
        name="in_proj_a",
    )(x2d, x2d, norm_w, w_t, conv_w, conv_b, tail0)


def _in_proj_b_kernel(xn_ref, wb_ref, wdt_ref, pb_ref, dt_ref):
    xn = xn_ref[...]
    pb_ref[...] = _dot_nt(xn, wb_ref[...]).astype(BF16)
    dt_ref[...] = _dot_nt(xn, wdt_ref[...])


def _in_proj_b(xn, w_b, w_dt, *, tm):
    rows = xn.shape[0]
    return pl.pallas_call(
        _in_proj_b_kernel,
        grid=(rows // tm,),
        in_specs=[
            pl.BlockSpec((tm, D_MODEL), lambda i: (i, 0)),
            pl.BlockSpec((D_PROJ_B, D_MODEL), lambda i: (0, 0)),
            pl.BlockSpec((DT_PAD, D_MODEL), lambda i: (0, 0)),
        ],
        out_specs=[
            pl.BlockSpec((tm, D_PROJ_B), lambda i: (i, 0)),
            pl.BlockSpec((tm, DT_PAD), lambda i: (i, 0)),
        ],
        out_shape=[
            jax.ShapeDtypeStruct((rows, D_PROJ_B), BF16),
            jax.ShapeDtypeStruct((rows, DT_PAD), F32),
        ],
        compiler_params=pltpu.CompilerParams(
            dimension_semantics=("arbitrary",), vmem_limit_bytes=VMEM_LIMIT),
        name="in_proj_b",
    )(xn, w_b, w_dt)


def _dt_cumsum(dt_ref, row0, dtb_ref, alog_ref, valid):
    dtr = dt_ref[pl.ds(row0, CHUNK), :]
    xx = dtr + dtb_ref[...]
    dt = jnp.maximum(xx, 0.0) + jnp.log1p(jnp.exp(-jnp.abs(xx)))
    if valid is not None:
        dt = dt * valid
    dt = jnp.where(_lane_iota((CHUNK, LANES)) < SSD_HEADS, dt, 0.0)
    dta = dt * (-jnp.exp(alog_ref[...]) * LOG2E)
    rr = _row_iota((CHUNK, 3 * CHUNK))
    cc = _lane_iota((CHUNK, 3 * CHUNK)) & (CHUNK - 1)
    tril3 = jnp.where(rr >= cc, 1.0, 0.0).astype(BF16)
    acs = _dot(tril3, jnp.concatenate(_split_bf16(dta, 3), axis=0))
    return dt, acs


def _expand_heads(dt, acs, e_ref):
    a_hi, a_mid, a_lo = [p.astype(F32) for p in _split_bf16(acs, 3)]
    a_stack = (a_hi + pltpu.roll(a_mid, 32, 1) + pltpu.roll(a_lo, 64, 1)).astype(BF16)
    acs_x = _dot(a_stack, e_ref[...])
    d_hi, d_mid = [p.astype(F32) for p in _split_bf16(dt, 2)]
    d_stack = (d_hi + pltpu.roll(d_mid, 32, 1)).astype(BF16)
    dt_x = _dot(d_stack, e_ref[...])
    return dt_x, acs_x


def _state_update(g, xs_ref, bc_ref, rows, dt_x, acs_x, state_ref):
    gl = slice(g * GROUP_W, (g + 1) * GROUP_W)
    acs_g = acs_x[:, gl]
    a_last = acs_g[CHUNK - 1:CHUNK, :]
    x_g = xs_ref[rows, gl].astype(F32)
    x_dt = x_g * dt_x[:, gl]
    xw = (x_dt * jnp.exp2(a_last - acs_g)).astype(BF16)
    b_g = bc_ref[rows, g * D_STATE:(g + 1) * D_STATE]
    state_ref[g] = state_ref[g] * jnp.exp2(a_last) + _dot_tn(b_g, xw)
    return x_g, x_dt


def _rope_kv(k_ref, v_ref, row0, ck_ref, sk_ref, kk_ref, vv_ref, slot):
    rows = pl.ds(row0, CHUNK)
    lane = _lane_iota((CHUNK, LANES))
    first_of_pair = (lane & 63) < 32
    low_half = lane < CHUNK
    ck = ck_ref[rows, :]
    sk = sk_ref[rows, :]
    dst = slice(slot * CHUNK, (slot + 1) * CHUNK)
    for c in range(2):
        kx = k_ref[rows, c * LANES:(c + 1) * LANES].astype(F32)
        kr = kx * ck + pltpu.roll(kx, 64, 1) * sk
        kk_ref[2 * c, dst, :] = jnp.where(first_of_pair, kr, pltpu.roll(kr, 32, 1)).astype(BF16)
        kk_ref[2 * c + 1, dst, :] = jnp.where(first_of_pair, pltpu.roll(kr, 96, 1), kr).astype(BF16)
        vx = v_ref[rows, c * LANES:(c + 1) * LANES].astype(F32)
        vr = pltpu.roll(vx, 64, 1)
        vv_ref[2 * c, dst, 0:LANES] = jnp.where(low_half, vx, vr).astype(BF16)
        vv_ref[2 * c + 1, dst, 0:LANES] = jnp.where(low_half, vr, vx).astype(BF16)


def _chunk(refs, row0, chunk_idx):
    (z_ref, xs_ref, bc_ref, q_ref, g_ref, k_ref, v_ref, dt_ref, cq_ref, sq_ref, ck_ref, sk_ref,
     dtb_ref, alog_ref, dskip_ref, nw_ref, e_ref, sinks_ref, out_ref,
     state_ref, xbd_ref, kk_ref, vv_ref) = refs
    rows = pl.ds(row0, CHUNK)
    lane = _lane_iota((CHUNK, LANES))
    row = _row_iota((CHUNK, LANES))
    first_of_pair = (lane & 63) < 32
    low_half = lane < CHUNK
    s_idx = lane & (CHUNK - 1)
    diag = row == s_idx
    causal = row >= s_idx

    dt, acs = _dt_cumsum(dt_ref, row0, dtb_ref, alog_ref, None)

    kk_ref[:, 0:2 * CHUNK, :] = kk_ref[:, CHUNK:BAND, :]
    vv_ref[:, 0:2 * CHUNK, 0:LANES] = vv_ref[:, CHUNK:BAND, 0:LANES]
    _rope_kv(k_ref, v_ref, row0, ck_ref, sk_ref, kk_ref, vv_ref, WINDOW_CHUNKS)
    key_abs = (chunk_idx - WINDOW_CHUNKS) * CHUNK + _lane_iota((1, BAND))
    bias = jnp.where(key_abs >= PAD_LEAD, 0.0, NEG_BIG)
    cq = cq_ref[rows, :]
    sq = sq_ref[rows, :]
    scores = []
    for kv in range(ATT_KV_HEADS):
        q_rows = []
        for c in (2 * kv, 2 * kv + 1):
            qx = q_ref[rows, c * LANES:(c + 1) * LANES].astype(F32)
            qr = qx * cq + pltpu.roll(qx, 64, 1) * sq
            q_rows.append(jnp.where(first_of_pair, qr, 0.0).astype(BF16))
            q_rows.append(jnp.where(first_of_pair, 0.0, qr).astype(BF16))
        q_stack = jnp.concatenate(q_rows, axis=0)
        scores.append(_dot_nt(q_stack, kk_ref[kv]) + bias)

    cb2, y_off_raw = [], []
    for g in range(SSD_GROUPS):
        nl = slice(g * D_STATE, (g + 1) * D_STATE)
        b_g = bc_ref[rows, nl]
        c_g = bc_ref[rows, D_BC // 2 + g * D_STATE:D_BC // 2 + (g + 1) * D_STATE]
        cb2.append(_dot_nt(c_g, jnp.concatenate([b_g, b_g], axis=0)))
        y_off_raw.append(_dot(c_g, state_ref[g].astype(BF16)))

    dt_x, acs_x = _expand_heads(dt, acs, e_ref)

    pvs, e_sinks = [], []
    for kv in range(ATT_KV_HEADS):
        s = scores[kv]
        p_rows, e_sink = [], []
        for r in range(ATT_REP):
            s_r = s[r * CHUNK:(r + 1) * CHUNK, :]
            sink = sinks_ref[kv * ATT_REP + r] * LOG2E
            m = jnp.maximum(jnp.max(s_r, axis=-1, keepdims=True), sink)
            p_rows.append(jnp.exp2(s_r - m).astype(BF16))
            e_sink.append(jnp.exp2(sink - m))
        pvs.append(_dot(jnp.concatenate(p_rows, axis=0), vv_ref[kv]))
        e_sinks.append(e_sink)

    xs_g, y_diag = [], []
    for g in range(SSD_GROUPS):
        x_g, x_dt = _state_update(g, xs_ref, bc_ref, rows, dt_x, acs_x, state_ref)
        xs_g.append(x_g)
        parts = []
        for pp in range(2):
            pls = slice(g * GROUP_W + pp * LANES, g * GROUP_W + (pp + 1) * LANES)
            col = acs_x[:, pls]
            rowv = jnp.sum(jnp.where(diag, col, 0.0), axis=0, keepdims=True)
            seg = jnp.where(causal, col - rowv, NEG_BIG)
            lmat = (jnp.exp2(seg) * cb2[g]).astype(BF16)
            xp = x_dt[:, pp * LANES:(pp + 1) * LANES].astype(BF16)
            p = 2 * g + pp
            xbd_ref[p, 0:CHUNK, 0:CHUNK] = xp[:, 0:CHUNK]
            xbd_ref[p, CHUNK:, CHUNK:] = xp[:, CHUNK:]
            parts.append(_dot(lmat, xbd_ref[p]))
        y_diag.append(parts)

    for kv in range(ATT_KV_HEADS):
        pv = pvs[kv]
        for pr in range(2):
            r0, r1 = 2 * pr, 2 * pr + 1
            blk0 = pv[r0 * CHUNK:(r0 + 1) * CHUNK, :]
            blk1 = pv[r1 * CHUNK:(r1 + 1) * CHUNK, :]
            num = jnp.where(low_half, blk0[:, 0:LANES], blk1[:, 0:LANES])
            den = jnp.where(low_half, blk0[:, LANES:] + e_sinks[kv][r0], blk1[:, LANES:] + e_sinks[kv][r1])
            col = (2 * kv + pr) * LANES
            gate = _silu(g_ref[rows, col:col + LANES].astype(F32))
            out_ref[rows, D_SSD + col:D_SSD + col + LANES] = (num / den * gate).astype(BF16)

    for g in range(SSD_GROUPS):
        gl = slice(g * GROUP_W, (g + 1) * GROUP_W)
        y = (jnp.concatenate(y_diag[g], axis=1) + y_off_raw[g] * jnp.exp2(acs_x[:, gl])
             + dskip_ref[:, gl] * xs_g[g])
        yz = y * z_ref[rows, gl].astype(F32)
        ms = jnp.mean(yz * yz, axis=-1, keepdims=True)
        out_ref[rows, gl] = (yz * lax.rsqrt(ms + EPS) * nw_ref[:, gl]).astype(BF16)


def _meta_kernel(xs_ref, bc_ref, k_ref, v_ref, dt_ref, ck_ref, sk_ref,
                 dtb_ref, alog_ref, e_ref,
                 state_ref, kk_ref, vv_ref):
    state_ref[...] = jnp.zeros(state_ref.shape, F32)
    valid = jnp.where(_row_iota((CHUNK, LANES)) >= PAD_LEAD, 1.0, 0.0)
    dt, acs = _dt_cumsum(dt_ref, 0, dtb_ref, alog_ref, valid)
    dt_x, acs_x = _expand_heads(dt, acs, e_ref)
    for g in range(SSD_GROUPS):
        _state_update(g, xs_ref, bc_ref, pl.ds(0, CHUNK), dt_x, acs_x, state_ref)
    _rope_kv(k_ref, v_ref, 0, ck_ref, sk_ref, kk_ref, vv_ref, 0)


def _mixer_kernel(z_ref, xs_ref, bc_ref, q_ref, g_ref, k_ref, v_ref, dt_ref,
                  cq_ref, sq_ref, ck_ref, sk_ref,
                  state0_ref, kk0_ref, vv0_ref,
                  dtb_ref, alog_ref, dskip_ref, nw_ref, e_ref, sinks_ref,
                  out_ref,
                  state_ref, xbd_ref, kk_ref, vv_ref, *, chunks):
    j = pl.program_id(1)

    @pl.when(j == 0)
    def _():
        state_ref[...] = state0_ref[...]
        xbd_ref[...] = jnp.zeros(xbd_ref.shape, BF16)
        kk_ref[...] = jnp.zeros(kk_ref.shape, BF16)
        vv_ref[:, :, 0:LANES] = jnp.zeros((ATT_KV_HEADS, BAND, LANES), BF16)
        vv_ref[:, :, LANES:] = jnp.ones((ATT_KV_HEADS, BAND, LANES), BF16)
        kk_ref[:, 2 * CHUNK:BAND, :] = kk0_ref[...]
        vv_ref[:, 2 * CHUNK:BAND, 0:LANES] = vv0_ref[...]

    refs = (z_ref, xs_ref, bc_ref, q_ref, g_ref, k_ref, v_ref, dt_ref, cq_ref, sq_ref, ck_ref, sk_ref,
            dtb_ref, alog_ref, dskip_ref, nw_ref, e_ref, sinks_ref, out_ref,
            state_ref, xbd_ref, kk_ref, vv_ref)

    def body(ci, carry):
        row0 = pl.multiple_of(ci * CHUNK, CHUNK)
        _chunk(refs, row0, j * chunks + ci + 1)
        return carry

    lax.fori_loop(0, chunks, body, 0)


def _const_spec(shape):
    nd = len(shape)
    return pl.BlockSpec(shape, lambda *_: (0,) * nd)


def _meta_state(projm_a, projm_b, dtm, ckm, skm, dtb, alog, e_mat):
    col = lambda w, idx: pl.BlockSpec((CHUNK, w), lambda i: (0, idx))
    return pl.pallas_call(
        _meta_kernel,
        grid=(1,),
        in_specs=[
            col(D_SSD, 1), col(D_BC, 2), col(D_KV, 2 * D_ATT // D_KV), col(D_KV, 2 * D_ATT // D_KV + 1),
            _const_spec((CHUNK, DT_PAD)), _const_spec((CHUNK, LANES)), _const_spec((CHUNK, LANES)),
            _const_spec((1, DT_PAD)), _const_spec((1, DT_PAD)), _const_spec((LANES, D_SSD)),
        ],
        out_specs=[
            _const_spec((SSD_GROUPS, D_STATE, GROUP_W)),
            _const_spec((ATT_KV_HEADS, CHUNK, LANES)), _const_spec((ATT_KV_HEADS, CHUNK, LANES)),
        ],
        out_shape=[
            jax.ShapeDtypeStruct((SSD_GROUPS, D_STATE, GROUP_W), F32),
            jax.ShapeDtypeStruct((ATT_KV_HEADS, CHUNK, LANES), BF16),
            jax.ShapeDtypeStruct((ATT_KV_HEADS, CHUNK, LANES), BF16),
        ],
        compiler_params=pltpu.CompilerParams(
            dimension_semantics=("arbitrary",), vmem_limit_bytes=VMEM_LIMIT),
        name="meta_state",
    )(projm_a, projm_a, projm_b, projm_b, dtm, ckm, skm, dtb, alog, e_mat)


def _mixer(proj_a, proj_b, dtr, tables, meta_state, params, *, batch, seq, tb):
    cq, sq, ck, sk = tables
    state0, kk0, vv0 = meta_state
    dtb, alog, dskip_x, ssd_nw, e_mat, sinks = params
    nblk = seq // tb
    rows = batch * seq
    chunks = tb // CHUNK
    col = lambda w, idx: pl.BlockSpec((tb, w), lambda b, j: (b * nblk + j, idx))
    tab = pl.BlockSpec((tb, LANES), lambda b, j: (j, 0))
    return pl.pallas_call(
        functools.partial(_mixer_kernel, chunks=chunks),
        grid=(batch, nblk),
        in_specs=[
            col(D_SSD, 0), col(D_SSD, 1), col(D_BC, 2), col(D_ATT, 0), col(D_ATT, 1),
            col(D_KV, 2 * D_ATT // D_KV), col(D_KV, 2 * D_ATT // D_KV + 1), col(DT_PAD, 0),
            tab, tab, tab, tab,
            _const_spec((SSD_GROUPS, D_STATE, GROUP_W)),
            _const_spec((ATT_KV_HEADS, CHUNK, LANES)), _const_spec((ATT_KV_HEADS, CHUNK, LANES)),
            _const_spec((1, DT_PAD)), _const_spec((1, DT_PAD)),
            _const_spec((1, D_SSD)), _const_spec((1, D_SSD)), _const_spec((LANES, D_SSD)),
            pl.BlockSpec(memory_space=pltpu.SMEM),
        ],
        out_specs=pl.BlockSpec((tb, D_MIX), lambda b, j: (b * nblk + j, 0)),
        out_shape=jax.ShapeDtypeStruct((rows, D_MIX), BF16),
        scratch_shapes=[
            pltpu.VMEM((SSD_GROUPS, D_STATE, GROUP_W), F32),
            pltpu.VMEM((SSD_HEADS // 2, 2 * CHUNK, LANES), BF16),
            pltpu.VMEM((ATT_KV_HEADS, BAND, LANES), BF16),
            pltpu.VMEM((ATT_KV_HEADS, BAND, 2 * LANES), BF16),
        ],
        compiler_params=pltpu.CompilerParams(
            dimension_semantics=("arbitrary", "arbitrary"), vmem_limit_bytes=VMEM_LIMIT),
        name="mixer",
    )(proj_a, proj_a, proj_a, proj_b, proj_b, proj_b, proj_b, dtr, cq, sq, ck, sk,
      state0, kk0, vv0, dtb, alog, dskip_x, ssd_nw, e_mat, sinks)


def _out_proj_kernel(mix_ref, w_ref, x_ref, nw_ref, o_ref):
    o = _dot(mix_ref[...], w_ref[...])
    ms = jnp.mean(o * o, axis=-1, keepdims=True)
    o_ref[...] = x_ref[...] + o * lax.rsqrt(ms + EPS) * nw_ref[...]


def _out_proj(mix, w_out, x2d, norm_w, *, tm):
    rows = mix.shape[0]
    return pl.pallas_call(
        _out_proj_kernel,
        grid=(rows // tm,),
        in_specs=[
            pl.BlockSpec((tm, D_MIX), lambda i: (i, 0)),
            pl.BlockSpec((D_MIX, D_MODEL), lambda i: (0, 0)),
            pl.BlockSpec((tm, D_MODEL), lambda i: (i, 0)),
            pl.BlockSpec((1, D_MODEL), lambda i: (0, 0)),
        ],
        out_specs=pl.BlockSpec((tm, D_MODEL), lambda i: (i, 0)),
        out_shape=jax.ShapeDtypeStruct((rows, D_MODEL), F32),
        compiler_params=pltpu.CompilerParams(
            dimension_semantics=("arbitrary",), vmem_limit_bytes=VMEM_LIMIT),
        name="out_proj",
    )(mix, w_out, x2d, norm_w)


def _pair_layout(w_rows, heads):
    half = ATT_HEAD_DIM // 2
    w = w_rows.reshape(heads // 2, 2, 2, half, w_rows.shape[1])
    return w.transpose(0, 2, 1, 3, 4).reshape(heads * ATT_HEAD_DIM, w_rows.shape[1])


def _rope_tables(n_pos):
    half = ATT_HEAD_DIM // 2
    pos = jnp.arange(n_pos, dtype=jnp.int32) - PAD_LEAD
    inv = ROPE_THETA ** (-jnp.arange(half, dtype=F32) / half)
    ang = pos.astype(F32)[:, None] * inv[None, :]
    cos, sin = jnp.cos(ang), jnp.sin(ang)
    cos4 = jnp.concatenate([cos, cos, cos, cos], axis=1)
    sin4 = jnp.concatenate([-sin, -sin, sin, sin], axis=1)
    return cos4, sin4


def kernel(x, meta_tokens, norm_pre_w, w_in, conv_w, conv_b, dt_bias, a_log, d_skip, ssd_norm_w,
           attn_sinks, w_out, norm_post_w):
    batch, seq, _ = x.shape
    assert norm_pre_w.shape[0] == 1 and seq % CHUNK == 0
    rows = batch * seq
    x2d = x.reshape(rows, D_MODEL)

    w_t = jnp.swapaxes(w_in[0], 0, 1).astype(BF16)
    o = D_PROJ_A
    seg = {}
    for name, width in (("dt", SSD_HEADS), ("q", D_ATT), ("k", D_KV), ("v", D_KV), ("g", D_ATT)):
        seg[name] = w_t[o:o + width]
        o += width
    w_b = jnp.concatenate([_pair_layout(seg["q"], ATT_Q_HEADS), seg["g"],
                           _pair_layout(seg["k"], ATT_KV_HEADS), seg["v"]], axis=0)
    w_dt = jnp.pad(seg["dt"], ((0, DT_PAD - SSD_HEADS), (0, 0)))
    w_o = w_out[0].astype(BF16)

    pad_h = lambda v: jnp.pad(v.reshape(1, SSD_HEADS), ((0, 0), (0, DT_PAD - SSD_HEADS)))
    dtb = pad_h(dt_bias[0])
    alog = pad_h(a_log[0])
    dskip_x = jnp.repeat(d_skip[0], SSD_HEAD_DIM).reshape(1, D_SSD)
    e_rows = jnp.arange(LANES)
    e_mat = ((e_rows[:, None] % SSD_HEADS == jnp.arange(D_SSD)[None, :] // SSD_HEAD_DIM)
             & (e_rows[:, None] < 3 * SSD_HEADS)).astype(BF16)

    cos4, sin4 = _rope_tables(CHUNK + seq)
    scale = ATT_HEAD_DIM ** -0.5 * LOG2E
    tables = (cos4[CHUNK:] * scale, sin4[CHUNK:] * scale, cos4[CHUNK:], sin4[CHUNK:])

    npw = norm_pre_w[0].reshape(1, D_MODEL)
    cw, cb = conv_w[0], conv_b[0].reshape(1, D_CONV)

    xm = jnp.concatenate([jnp.zeros((PAD_LEAD, D_MODEL), x.dtype), meta_tokens.astype(x.dtype)], axis=0)
    n_tail = (D_PROJ_A // A_TILE - 1) * A_LANE_TILES
    no_tail = jnp.zeros((n_tail, 8, LANES), F32)
    projm_a, xnm, tail_m = _in_proj_a(xm, npw, w_t, cw, cb, no_tail, tm=CHUNK, seq=CHUNK)
    projm_b, dtm = _in_proj_b(xnm, w_b, w_dt, tm=CHUNK)
    meta_state = _meta_state(projm_a, projm_b, dtm, cos4[:CHUNK], sin4[:CHUNK], dtb, alog, e_mat)

    proj_a, xn, _ = _in_proj_a(x2d, npw, w_t, cw, cb, tail_m.reshape(n_tail, 8, LANES), tm=TM_A, seq=seq)
    proj_b, dtr = _in_proj_b(xn, w_b, w_dt, tm=TM_B)
    params = (dtb, alog, dskip_x, ssd_norm_w[0].reshape(1, D_SSD), e_mat, attn_sinks[0])
    mix = _mixer(proj_a, proj_b, dtr, tables, meta_state, params, batch=batch, seq=seq, tb=TB_MIX)
    out = _out_proj(mix, w_o, x2d, norm_post_w[0].reshape(1, D_MODEL), tm=TM_OUT)
    return out.reshape(batch, seq, D_MODEL)
```

```python
import functools

import jax
import jax.numpy as jnp
from jax import lax
from jax.experimental import pallas as pl
from jax.experimental.pallas import tpu as pltpu

D_MODEL = 2048
CHUNK = 64
N_META = 16
PAD_LEAD = CHUNK - N_META
EPS = 1e-6

SSD_HEADS = 32
SSD_HEAD_DIM = 64
D_SSD = SSD_HEADS * SSD_HEAD_DIM
SSD_GROUPS = 8
D_STATE = 128
CONV_WIDTH = 4
D_BC = 2 * SSD_GROUPS * D_STATE
D_CONV = D_SSD + D_BC
GROUP_W = D_SSD // SSD_GROUPS

ATT_Q_HEADS = 16
ATT_KV_HEADS = 4
ATT_REP = ATT_Q_HEADS // ATT_KV_HEADS
ATT_HEAD_DIM = 64
D_ATT = ATT_Q_HEADS * ATT_HEAD_DIM
D_KV = ATT_KV_HEADS * ATT_HEAD_DIM
WINDOW_CHUNKS = 2
BAND = (WINDOW_CHUNKS + 1) * CHUNK
ROPE_THETA = 10000.0
D_MIX = D_SSD + D_ATT

LANES = 128
DT_PAD = LANES
D_PROJ_A = D_SSD + D_SSD + D_BC
D_PROJ_B = D_ATT + D_ATT + D_KV + D_KV
NEG_BIG = -1e30
LOG2E = 1.4426950408889634
VMEM_LIMIT = 56 * 1024 * 1024
TM_A = 512
TM_B = 1024
TB_MIX = 512
TM_OUT = 512
N_SPLIT = 2

F32 = jnp.float32
BF16 = jnp.bfloat16


def _dot(a, b):
    return jnp.dot(a, b, preferred_element_type=F32)


def _dot_nt(a, b):
    return lax.dot_general(a, b, (((1,), (1,)), ((), ())), preferred_element_type=F32)


def _dot_tn(a, b):
    return lax.dot_general(a, b, (((0,), (0,)), ((), ())), preferred_element_type=F32)


def _silu(x):
    return x / (1.0 + jnp.exp2(x * (-LOG2E)))


def _split_bf16(x, parts):
    out = []
    r = x
    for _ in range(parts):
        p = r.astype(BF16)
        out.append(p)
        r = r - p.astype(F32)
    return out


def _lane_iota(shape):
    return lax.broadcasted_iota(jnp.int32, shape, 1)


def _row_iota(shape):
    return lax.broadcasted_iota(jnp.int32, shape, 0)


A_TILE = D_SSD
A_LANE_TILES = A_TILE // LANES


def _in_proj_a_kernel(x0_ref, xnext_ref, nw_ref, w_ref, convw_ref, convb_ref, tail0_ref,
                      pa_ref, xn_out_ref, tail_ref, xn_cur_ref, xn_next_ref, cbuf_ref, *, strip, tiles_per_seq):
    i = pl.program_id(0)
    j = pl.program_id(1)
    tm = xnext_ref.shape[0]

    def norm_rows(src_ref, r0, nrows):
        xv = src_ref[pl.ds(r0, nrows), :]
        ms = jnp.mean(xv * xv, axis=-1, keepdims=True)
        xn_next_ref[pl.ds(r0, nrows), :] = (xv * lax.rsqrt(ms + EPS) * nw_ref[...]).astype(BF16)

    @pl.when((i == 0) & (j == 0))
    def _():
        def body(s, carry):
            norm_rows(x0_ref, pl.multiple_of(s * CHUNK, CHUNK), CHUNK)
            return carry
        lax.fori_loop(0, tm // CHUNK, body, 0)

    @pl.when((i % tiles_per_seq == 0) & (j > 0))
    def _():
        tail_ref[j - 1] = tail0_ref[...]

    def norm_ahead():
        r0 = pl.multiple_of(jnp.minimum(j * strip, tm - strip), CHUNK)
        norm_rows(xnext_ref, r0, strip)

    @pl.when(j == 0)
    def _():
        xn_cur = xn_next_ref[...]
        xn_cur_ref[...] = xn_cur
        xn_out_ref[...] = xn_cur
        norm_ahead()
        pa_ref[...] = _silu(_dot_nt(xn_cur, w_ref[...])).astype(BF16)

    @pl.when(j > 0)
    def _():
        norm_ahead()
        acc = _dot_nt(xn_cur_ref[...], w_ref[...])
        for t in range(A_LANE_TILES):
            cols = slice(t * LANES, (t + 1) * LANES)
            cbuf_ref[t, 0:8, :] = tail_ref[j - 1, t]
            cbuf_ref[t, 8:8 + tm, :] = acc[:, cols]
            conv = convb_ref[:, cols] + convw_ref[0:1, cols] * cbuf_ref[t, 5:5 + tm, :]
            for tap in range(1, CONV_WIDTH):
                conv = conv + convw_ref[tap:tap + 1, cols] * cbuf_ref[t, 5 + tap:5 + tap + tm, :]
            pa_ref[:, cols] = _silu(conv).astype(BF16)
            tail_ref[j - 1, t] = cbuf_ref[t, tm:tm + 8, :]


def _in_proj_a(x2d, norm_w, w_t, conv_w, conv_b, tail0, *, tm, seq):
    rows = x2d.shape[0]
    tn = A_TILE
    nrt, nct = rows // tm, D_PROJ_A // tn
    strip = min(tm, -(-tm // nct // CHUNK) * CHUNK)
    assert strip * nct >= tm and strip <= tm and tm % CHUNK == 0 and seq % tm == 0
    last = nrt - 1
    conv_col = lambda i, j: (0, jnp.maximum(j - 1, 0))
    return pl.pallas_call(
        functools.partial(_in_proj_a_kernel, strip=strip, tiles_per_seq=seq // tm),
        grid=(nrt, nct),
        in_specs=[
            pl.BlockSpec((tm, D_MODEL), lambda i, j: (0, 0)),
            pl.BlockSpec((tm, D_MODEL), lambda i, j: (jnp.minimum(i + 1, last), 0)),
            pl.BlockSpec((1, D_MODEL), lambda i, j: (0, 0)),
            pl.BlockSpec((tn, D_MODEL), lambda i, j: (j, 0)),
            pl.BlockSpec((CONV_WIDTH, tn), conv_col),
            pl.BlockSpec((1, tn), conv_col),
            pl.BlockSpec((A_LANE_TILES, 8, LANES), lambda i, j: (jnp.maximum(j - 1, 0), 0, 0)),
        ],
        out_specs=[
            pl.BlockSpec((tm, tn), lambda i, j: (i, j)),
            pl.BlockSpec((tm, D_MODEL), lambda i, j: (i, 0)),
            pl.BlockSpec((nct - 1, A_LANE_TILES, 8, LANES), lambda i, j: (0, 0, 0, 0)),
        ],
        out_shape=[
            jax.ShapeDtypeStruct((rows, D_PROJ_A), BF16),
            jax.ShapeDtypeStruct((rows, D_MODEL), BF16),
            jax.ShapeDtypeStruct((nct - 1, A_LANE_TILES, 8, LANES), F32),
        ],
        scratch_shapes=[pltpu.VMEM((tm, D_MODEL), BF16),
                        pltpu.VMEM((tm, D_MODEL), BF16),
                        pltpu.VMEM((A_LANE_TILES, tm + 8, LANES), F32)],
        compiler_params=pltpu.CompilerParams(
            dimension_semantics=("arbitrary", "arbitrary"), vmem_limit_bytes=VMEM_LIMIT),
        name="in_proj_a",
    )(x2d, x2d, norm_w, w_t, conv_w, conv_b, tail0)


def _in_proj_b_kernel(xn_ref, wb_ref, wdt_ref, pb_ref, dt_ref):
    xn = xn_ref[...]
    pb_ref[...] = _dot_nt(xn, wb_ref[...]).astype(BF16)
    dt_ref[...] = _dot_nt(xn, wdt_ref[...])


def _in_proj_b(xn, w_b, w_dt, *, tm):
    rows = xn.shape[0]
    return pl.pallas_call(
        _in_proj_b_kernel,
        grid=(rows // tm,),
        in_specs=[
            pl.BlockSpec((tm, D_MODEL), lambda i: (i, 0)),
            pl.BlockSpec((D_PROJ_B, D_MODEL), lambda i: (0, 0)),
            pl.BlockSpec((DT_PAD, D_MODEL), lambda i: (0, 0)),
        ],
        out_specs=[
            pl.BlockSpec((tm, D_PROJ_B), lambda i: (i, 0)),
            pl.BlockSpec((tm, DT_PAD), lambda i: (i, 0)),
        ],
        out_shape=[
            jax.ShapeDtypeStruct((rows, D_PROJ_B), BF16),
            jax.ShapeDtypeStruct((rows, DT_PAD), F32),
        ],
        compiler_params=pltpu.CompilerParams(
            dimension_semantics=("arbitrary",), vmem_limit_bytes=VMEM_LIMIT),
        name="in_proj_b",
    )(xn, w_b, w_dt)


def _dt_cumsum(dt_ref, row0, dtb_ref, alog_ref, valid):
    dtr = dt_ref[pl.ds(row0, CHUNK), :]
    xx = dtr + dtb_ref[...]
    dt = jnp.maximum(xx, 0.0) + jnp.log1p(jnp.exp(-jnp.abs(xx)))
    if valid is not None:
        dt = dt * valid
    dt = jnp.where(_lane_iota((CHUNK, LANES)) < SSD_HEADS, dt, 0.0)
    dta = dt * (-jnp.exp(alog_ref[...]) * LOG2E)
    rr = _row_iota((CHUNK, 3 * CHUNK))
    cc = _lane_iota((CHUNK, 3 * CHUNK)) & (CHUNK - 1)
    tril3 = jnp.where(rr >= cc, 1.0, 0.0).astype(BF16)
    acs = _dot(tril3, jnp.concatenate(_split_bf16(dta, 3), axis=0))
    return dt, acs


def _expand_heads(dt, acs, e_ref):
    a_hi, a_mid, a_lo = [p.astype(F32) for p in _split_bf16(acs, 3)]
    a_stack = (a_hi + pltpu.roll(a_mid, 32, 1) + pltpu.roll(a_lo, 64, 1)).astype(BF16)
    acs_x = _dot(a_stack, e_ref[...])
    d_hi, d_mid = [p.astype(F32) for p in _split_bf16(dt, 2)]
    d_stack = (d_hi + pltpu.roll(d_mid, 32, 1)).astype(BF16)
    dt_x = _dot(d_stack, e_ref[...])
    return dt_x, acs_x


def _state_update(g, xs_ref, bc_ref, rows, dt_x, acs_x, state_ref):
    gl = slice(g * GROUP_W, (g + 1) * GROUP_W)
    acs_g = acs_x[:, gl]
    a_last = acs_g[CHUNK - 1:CHUNK, :]
    x_g = xs_ref[rows, gl].astype(F32)
    x_dt = x_g * dt_x[:, gl]
    xw = (x_dt * jnp.exp2(a_last - acs_g)).astype(BF16)
    b_g = bc_ref[rows, g * D_STATE:(g + 1) * D_STATE]
    state_ref[g] = state_ref[g] * jnp.exp2(a_last) + _dot_tn(b_g, xw)
    return x_g, x_dt


def _rope_kv(k_ref, v_ref, row0, ck_ref, sk_ref, kk_ref, vv_ref, slot):
    rows = pl.ds(row0, CHUNK)
    lane = _lane_iota((CHUNK, LANES))
    first_of_pair = (lane & 63) < 32
    low_half = lane < CHUNK
    ck = ck_ref[rows, :]
    sk = sk_ref[rows, :]
    dst = slice(slot * CHUNK, (slot + 1) * CHUNK)
    for c in range(2):
        kx = k_ref[rows, c * LANES:(c + 1) * LANES].astype(F32)
        kr = kx * ck + pltpu.roll(kx, 64, 1) * sk
        kk_ref[2 * c, dst, :] = jnp.where(first_of_pair, kr, pltpu.roll(kr, 32, 1)).astype(BF16)
        kk_ref[2 * c + 1, dst, :] = jnp.where(first_of_pair, pltpu.roll(kr, 96, 1), kr).astype(BF16)
        vx = v_ref[rows, c * LANES:(c + 1) * LANES].astype(F32)
        vr = pltpu.roll(vx, 64, 1)
        vv_ref[2 * c, dst, 0:LANES] = jnp.where(low_half, vx, vr).astype(BF16)
        vv_ref[2 * c + 1, dst, 0:LANES] = jnp.where(low_half, vr, vx).astype(BF16)


def _chunk(refs, row0, chunk_idx):
    (z_ref, xs_ref, bc_ref, q_ref, g_ref, k_ref, v_ref, dt_ref, cq_ref, sq_ref, ck_ref, sk_ref,
     dtb_ref, alog_ref, dskip_ref, nw_ref, e_ref, sinks_ref, out_ref,
     state_ref, xbd_ref, kk_ref, vv_ref) = refs
    rows = pl.ds(row0, CHUNK)
    lane = _lane_iota((CHUNK, LANES))
    row = _row_iota((CHUNK, LANES))
    first_of_pair = (lane & 63) < 32
    low_half = lane < CHUNK
    s_idx = lane & (CHUNK - 1)
    diag = row == s_idx
    causal = row >= s_idx

    dt, acs = _dt_cumsum(dt_ref, row0, dtb_ref, alog_ref, None)

    kk_ref[:, 0:2 * CHUNK, :] = kk_ref[:, CHUNK:BAND, :]
    vv_ref[:, 0:2 * CHUNK, 0:LANES] = vv_ref[:, CHUNK:BAND, 0:LANES]
    _rope_kv(k_ref, v_ref, row0, ck_ref, sk_ref, kk_ref, vv_ref, WINDOW_CHUNKS)
    key_abs = (chunk_idx - WINDOW_CHUNKS) * CHUNK + _lane_iota((1, BAND))
    bias = jnp.where(key_abs >= PAD_LEAD, 0.0, NEG_BIG)
    cq = cq_ref[rows, :]
    sq = sq_ref[rows, :]
    expanded = []

    def run(kvs, gs):
        scores = {}
        for kv in kvs:
            q_rows = []
            for c in (2 * kv, 2 * kv + 1):
                qx = q_ref[rows, c * LANES:(c + 1) * LANES].astype(F32)
                qr = qx * cq + pltpu.roll(qx, 64, 1) * sq
                q_rows.append(jnp.where(first_of_pair, qr, 0.0).astype(BF16))
                q_rows.append(jnp.where(first_of_pair, 0.0, qr).astype(BF16))
            q_stack = jnp.concatenate(q_rows, axis=0)
            scores[kv] = _dot_nt(q_stack, kk_ref[kv]) + bias

        cb2, y_off_raw = {}, {}
        for g in gs:
            nl = slice(g * D_STATE, (g + 1) * D_STATE)
            b_g = bc_ref[rows, nl]
            c_g = bc_ref[rows, D_BC // 2 + g * D_STATE:D_BC // 2 + (g + 1) * D_STATE]
            cb2[g] = _dot_nt(c_g, jnp.concatenate([b_g, b_g], axis=0))
            y_off_raw[g] = _dot(c_g, state_ref[g].astype(BF16))

        if not expanded:
            expanded.extend(_expand_heads(dt, acs, e_ref))
        dt_x, acs_x = expanded

        pvs, e_sinks = {}, {}
        for kv in kvs:
            s = scores[kv]
            p_rows, e_sink = [], []
            for r in range(ATT_REP):
                s_r = s[r * CHUNK:(r + 1) * CHUNK, :]
                sink = sinks_ref[kv * ATT_REP + r] * LOG2E
                m = jnp.maximum(jnp.max(s_r, axis=-1, keepdims=True), sink)
                p_rows.append(jnp.exp2(s_r - m).astype(BF16))
                e_sink.append(jnp.exp2(sink - m))
            pvs[kv] = _dot(jnp.concatenate(p_rows, axis=0), vv_ref[kv])
            e_sinks[kv] = e_sink

        xs_g, y_diag = {}, {}
        for g in gs:
            x_g, x_dt = _state_update(g, xs_ref, bc_ref, rows, dt_x, acs_x, state_ref)
            xs_g[g] = x_g
            parts = []
            for pp in range(2):
                pls = slice(g * GROUP_W + pp * LANES, g * GROUP_W + (pp + 1) * LANES)
                col = acs_x[:, pls]
                rowv = jnp.sum(jnp.where(diag, col, 0.0), axis=0, keepdims=True)
                seg = jnp.where(causal, col - rowv, NEG_BIG)
                lmat = (jnp.exp2(seg) * cb2[g]).astype(BF16)
                xp = x_dt[:, pp * LANES:(pp + 1) * LANES].astype(BF16)
                p = 2 * g + pp
                xbd_ref[p, 0:CHUNK, 0:CHUNK] = xp[:, 0:CHUNK]
                xbd_ref[p, CHUNK:, CHUNK:] = xp[:, CHUNK:]
                parts.append(_dot(lmat, xbd_ref[p]))
            y_diag[g] = parts

        for kv in kvs:
            pv = pvs[kv]
            for pr in range(2):
                r0, r1 = 2 * pr, 2 * pr + 1
                blk0 = pv[r0 * CHUNK:(r0 + 1) * CHUNK, :]
                blk1 = pv[r1 * CHUNK:(r1 + 1) * CHUNK, :]
                num = jnp.where(low_half, blk0[:, 0:LANES], blk1[:, 0:LANES])
                den = jnp.where(low_half, blk0[:, LANES:] + e_sinks[kv][r0], blk1[:, LANES:] + e_sinks[kv][r1])
                col = (2 * kv + pr) * LANES
                gate = _silu(g_ref[rows, col:col + LANES].astype(F32))
                out_ref[rows, D_SSD + col:D_SSD + col + LANES] = (num / den * gate).astype(BF16)

        for g in gs:
            gl = slice(g * GROUP_W, (g + 1) * GROUP_W)
            y = (jnp.concatenate(y_diag[g], axis=1) + y_off_raw[g] * jnp.exp2(acs_x[:, gl])
                 + dskip_ref[:, gl] * xs_g[g])
            yz = y * z_ref[rows, gl].astype(F32)
            ms = jnp.mean(yz * yz, axis=-1, keepdims=True)
            out_ref[rows, gl] = (yz * lax.rsqrt(ms + EPS) * nw_ref[:, gl]).astype(BF16)

    for h in range(N_SPLIT):
        run(range(h * ATT_KV_HEADS // N_SPLIT, (h + 1) * ATT_KV_HEADS // N_SPLIT),
            range(h * SSD_GROUPS // N_SPLIT, (h + 1) * SSD_GROUPS // N_SPLIT))


def _meta_kernel(xs_ref, bc_ref, k_ref, v_ref, dt_ref, ck_ref, sk_ref,
                 dtb_ref, alog_ref, e_ref,
                 state_ref, kk_ref, vv_ref):
    state_ref[...] = jnp.zeros(state_ref.shape, F32)
    valid = jnp.where(_row_iota((CHUNK, LANES)) >= PAD_LEAD, 1.0, 0.0)
    dt, acs = _dt_cumsum(dt_ref, 0, dtb_ref, alog_ref, valid)
    dt_x, acs_x = _expand_heads(dt, acs, e_ref)
    for g in range(SSD_GROUPS):
        _state_update(g, xs_ref, bc_ref, pl.ds(0, CHUNK), dt_x, acs_x, state_ref)
    _rope_kv(k_ref, v_ref, 0, ck_ref, sk_ref, kk_ref, vv_ref, 0)


def _mixer_kernel(z_ref, xs_ref, bc_ref, q_ref, g_ref, k_ref, v_ref, dt_ref,
                  cq_ref, sq_ref, ck_ref, sk_ref,
                  state0_ref, kk0_ref, vv0_ref,
                  dtb_ref, alog_ref, dskip_ref, nw_ref, e_ref, sinks_ref,
                  out_ref,
                  state_ref, xbd_ref, kk_ref, vv_ref, *, chunks):
    j = pl.program_id(1)

    @pl.when(j == 0)
    def _():
        state_ref[...] = state0_ref[...]
        xbd_ref[...] = jnp.zeros(xbd_ref.shape, BF16)
        kk_ref[...] = jnp.zeros(kk_ref.shape, BF16)
        vv_ref[:, :, 0:LANES] = jnp.zeros((ATT_KV_HEADS, BAND, LANES), BF16)
        vv_ref[:, :, LANES:] = jnp.ones((ATT_KV_HEADS, BAND, LANES), BF16)
        kk_ref[:, 2 * CHUNK:BAND, :] = kk0_ref[...]
        vv_ref[:, 2 * CHUNK:BAND, 0:LANES] = vv0_ref[...]

    refs = (z_ref, xs_ref, bc_ref, q_ref, g_ref, k_ref, v_ref, dt_ref, cq_ref, sq_ref, ck_ref, sk_ref,
            dtb_ref, alog_ref, dskip_ref, nw_ref, e_ref, sinks_ref, out_ref,
            state_ref, xbd_ref, kk_ref, vv_ref)

    def body(ci, carry):
        row0 = pl.multiple_of(ci * CHUNK, CHUNK)
        _chunk(refs, row0, j * chunks + ci + 1)
        return carry

    lax.fori_loop(0, chunks, body, 0)


def _const_spec(shape):
    nd = len(shape)
    return pl.BlockSpec(shape, lambda *_: (0,) * nd)


def _meta_state(projm_a, projm_b, dtm, ckm, skm, dtb, alog, e_mat):
    col = lambda w, idx: pl.BlockSpec((CHUNK, w), lambda i: (0, idx))
    return pl.pallas_call(
        _meta_kernel,
        grid=(1,),
        in_specs=[
            col(D_SSD, 1), col(D_BC, 2), col(D_KV, 2 * D_ATT // D_KV), col(D_KV, 2 * D_ATT // D_KV + 1),
            _const_spec((CHUNK, DT_PAD)), _const_spec((CHUNK, LANES)), _const_spec((CHUNK, LANES)),
            _const_spec((1, DT_PAD)), _const_spec((1, DT_PAD)), _const_spec((LANES, D_SSD)),
        ],
        out_specs=[
            _const_spec((SSD_GROUPS, D_STATE, GROUP_W)),
            _const_spec((ATT_KV_HEADS, CHUNK, LANES)), _const_spec((ATT_KV_HEADS, CHUNK, LANES)),
        ],
        out_shape=[
            jax.ShapeDtypeStruct((SSD_GROUPS, D_STATE, GROUP_W), F32),
            jax.ShapeDtypeStruct((ATT_KV_HEADS, CHUNK, LANES), BF16),
            jax.ShapeDtypeStruct((ATT_KV_HEADS, CHUNK, LANES), BF16),
        ],
        compiler_params=pltpu.CompilerParams(
            dimension_semantics=("arbitrary",), vmem_limit_bytes=VMEM_LIMIT),
        name="meta_state",
    )(projm_a, projm_a, projm_b, projm_b, dtm, ckm, skm, dtb, alog, e_mat)


def _mixer(proj_a, proj_b, dtr, tables, meta_state, params, *, batch, seq, tb):
    cq, sq, ck, sk = tables
    state0, kk0, vv0 = meta_state
    dtb, alog, dskip_x, ssd_nw, e_mat, sinks = params
    nblk = seq // tb
    rows = batch * seq
    chunks = tb // CHUNK
    col = lambda w, idx: pl.BlockSpec((tb, w), lambda b, j: (b * nblk + j, idx))
    tab = pl.BlockSpec((tb, LANES), lambda b, j: (j, 0))
    return pl.pallas_call(
        functools.partial(_mixer_kernel, chunks=chunks),
        grid=(batch, nblk),
        in_specs=[
            col(D_SSD, 0), col(D_SSD, 1), col(D_BC, 2), col(D_ATT, 0), col(D_ATT, 1),
            col(D_KV, 2 * D_ATT // D_KV), col(D_KV, 2 * D_ATT // D_KV + 1), col(DT_PAD, 0),
            tab, tab, tab, tab,
            _const_spec((SSD_GROUPS, D_STATE, GROUP_W)),
            _const_spec((ATT_KV_HEADS, CHUNK, LANES)), _const_spec((ATT_KV_HEADS, CHUNK, LANES)),
            _const_spec((1, DT_PAD)), _const_spec((1, DT_PAD)),
            _const_spec((1, D_SSD)), _const_spec((1, D_SSD)), _const_spec((LANES, D_SSD)),
            pl.BlockSpec(memory_space=pltpu.SMEM),
        ],
        out_specs=pl.BlockSpec((tb, D_MIX), lambda b, j: (b * nblk + j, 0)),
        out_shape=jax.ShapeDtypeStruct((rows, D_MIX), BF16),
        scratch_shapes=[
            pltpu.VMEM((SSD_GROUPS, D_STATE, GROUP_W), F32),
            pltpu.VMEM((SSD_HEADS // 2, 2 * CHUNK, LANES), BF16),
            pltpu.VMEM((ATT_KV_HEADS, BAND, LANES), BF16),
            pltpu.VMEM((ATT_KV_HEADS, BAND, 2 * LANES), BF16),
        ],
        compiler_params=pltpu.CompilerParams(
            dimension_semantics=("arbitrary", "arbitrary"), vmem_limit_bytes=VMEM_LIMIT),
        name="mixer",
    )(proj_a, proj_a, proj_a, proj_b, proj_b, proj_b, proj_b, dtr, cq, sq, ck, sk,
      state0, kk0, vv0, dtb, alog, dskip_x, ssd_nw, e_mat, sinks)


def _out_proj_kernel(mix_ref, w_ref, x_ref, nw_ref, o_ref):
    o = _dot(mix_ref[...], w_ref[...])
    ms = jnp.mean(o * o, axis=-1, keepdims=True)
    o_ref[...] = x_ref[...] + o * lax.rsqrt(ms + EPS) * nw_ref[...]


def _out_proj(mix, w_out, x2d, norm_w, *, tm):
    rows = mix.shape[0]
    return pl.pallas_call(
        _out_proj_kernel,
        grid=(rows // tm,),
        in_specs=[
            pl.BlockSpec((tm, D_MIX), lambda i: (i, 0)),
            pl.BlockSpec((D_MIX, D_MODEL), lambda i: (0, 0)),
            pl.BlockSpec((tm, D_MODEL), lambda i: (i, 0)),
            pl.BlockSpec((1, D_MODEL), lambda i: (0, 0)),
        ],
        out_specs=pl.BlockSpec((tm, D_MODEL), lambda i: (i, 0)),
        out_shape=jax.ShapeDtypeStruct((rows, D_MODEL), F32),
        compiler_params=pltpu.CompilerParams(
            dimension_semantics=("arbitrary",), vmem_limit_bytes=VMEM_LIMIT),
        name="out_proj",
    )(mix, w_out, x2d, norm_w)


def _pair_layout(w_rows, heads):
    half = ATT_HEAD_DIM // 2
    w = w_rows.reshape(heads // 2, 2, 2, half, w_rows.shape[1])
    return w.transpose(0, 2, 1, 3, 4).reshape(heads * ATT_HEAD_DIM, w_rows.shape[1])


def _rope_tables(n_pos):
    half = ATT_HEAD_DIM // 2
    pos = jnp.arange(n_pos, dtype=jnp.int32) - PAD_LEAD
    inv = ROPE_THETA ** (-jnp.arange(half, dtype=F32) / half)
    ang = pos.astype(F32)[:, None] * inv[None, :]
    cos, sin = jnp.cos(ang), jnp.sin(ang)
    cos4 = jnp.concatenate([cos, cos, cos, cos], axis=1)
    sin4 = jnp.concatenate([-sin, -sin, sin, sin], axis=1)
    return cos4, sin4


def kernel(x, meta_tokens, norm_pre_w, w_in, conv_w, conv_b, dt_bias, a_log, d_skip, ssd_norm_w,
           attn_sinks, w_out, norm_post_w):
    batch, seq, _ = x.shape
    assert norm_pre_w.shape[0] == 1 and seq % CHUNK == 0
    rows = batch * seq
    x2d = x.reshape(rows, D_MODEL)

    w_t = jnp.swapaxes(w_in[0], 0, 1).astype(BF16)
    o = D_PROJ_A
    seg = {}
    for name, width in (("dt", SSD_HEADS), ("q", D_ATT), ("k", D_KV), ("v", D_KV), ("g", D_ATT)):
        seg[name] = w_t[o:o + width]
        o += width
    w_b = jnp.concatenate([_pair_layout(seg["q"], ATT_Q_HEADS), seg["g"],
                           _pair_layout(seg["k"], ATT_KV_HEADS), seg["v"]], axis=0)
    w_dt = jnp.pad(seg["dt"], ((0, DT_PAD - SSD_HEADS), (0, 0)))
    w_o = w_out[0].astype(BF16)

    pad_h = lambda v: jnp.pad(v.reshape(1, SSD_HEADS), ((0, 0), (0, DT_PAD - SSD_HEADS)))
    dtb = pad_h(dt_bias[0])
    alog = pad_h(a_log[0])
    dskip_x = jnp.repeat(d_skip[0], SSD_HEAD_DIM).reshape(1, D_SSD)
    e_rows = jnp.arange(LANES)
    e_mat = ((e_rows[:, None] % SSD_HEADS == jnp.arange(D_SSD)[None, :] // SSD_HEAD_DIM)
             & (e_rows[:, None] < 3 * SSD_HEADS)).astype(BF16)

    cos4, sin4 = _rope_tables(CHUNK + seq)
    scale = ATT_HEAD_DIM ** -0.5 * LOG2E
    tables = (cos4[CHUNK:] * scale, sin4[CHUNK:] * scale, cos4[CHUNK:], sin4[CHUNK:])

    npw = norm_pre_w[0].reshape(1, D_MODEL)
    cw, cb = conv_w[0], conv_b[0].reshape(1, D_CONV)

    xm = jnp.concatenate([jnp.zeros((PAD_LEAD, D_MODEL), x.dtype), meta_tokens.astype(x.dtype)], axis=0)
    n_tail = (D_PROJ_A // A_TILE - 1) * A_LANE_TILES
    no_tail = jnp.zeros((n_tail, 8, LANES), F32)
    projm_a, xnm, tail_m = _in_proj_a(xm, npw, w_t, cw, cb, no_tail, tm=CHUNK, seq=CHUNK)
    projm_b, dtm = _in_proj_b(xnm, w_b, w_dt, tm=CHUNK)
    meta_state = _meta_state(projm_a, projm_b, dtm, cos4[:CHUNK], sin4[:CHUNK], dtb, alog, e_mat)

    proj_a, xn, _ = _in_proj_a(x2d, npw, w_t, cw, cb, tail_m.reshape(n_tail, 8, LANES), tm=TM_A, seq=seq)
    proj_b, dtr = _in_proj_b(xn, w_b, w_dt, tm=TM_B)
    params = (dtb, alog, dskip_x, ssd_norm_w[0].reshape(1, D_SSD), e_mat, attn_sinks[0])
    mix = _mixer(proj_a, proj_b, dtr, tables, meta_state, params, batch=batch, seq=seq, tb=TB_MIX)
    out = _out_proj(mix, w_o, x2d, norm_post_w[0].reshape(1, D_MODEL), tm=TM_OUT)
    return out.reshape(batch, seq, D_MODEL)
```

```python
import functools

import jax
import jax.numpy as jnp
from jax import lax
from jax.experimental import pallas as pl
from jax.experimental.pallas import tpu as pltpu

D_MODEL = 2048
CHUNK = 64
N_META = 16
PAD_LEAD = CHUNK - N_META
EPS = 1e-6

SSD_HEADS = 32
SSD_HEAD_DIM = 64
D_SSD = SSD_HEADS * SSD_HEAD_DIM
SSD_GROUPS = 8
D_STATE = 128
CONV_WIDTH = 4
D_BC = 2 * SSD_GROUPS * D_STATE
D_CONV = D_SSD + D_BC
GROUP_W = D_SSD // SSD_GROUPS

ATT_Q_HEADS = 16
ATT_KV_HEADS = 4
ATT_REP = ATT_Q_HEADS // ATT_KV_HEADS
ATT_HEAD_DIM = 64
D_ATT = ATT_Q_HEADS * ATT_HEAD_DIM
D_KV = ATT_KV_HEADS * ATT_HEAD_DIM
WINDOW_CHUNKS = 2
BAND = (WINDOW_CHUNKS + 1) * CHUNK
ROPE_THETA = 10000.0
D_MIX = D_SSD + D_ATT

LANES = 128
DT_PAD = LANES
D_PROJ_A = D_SSD + D_SSD + D_BC
D_PROJ_B = D_ATT + D_ATT + D_KV + D_KV
NEG_BIG = -1e30
LOG2E = 1.4426950408889634
VMEM_LIMIT = 56 * 1024 * 1024
TM_A = 512
TM_B = 1024
TB_MIX = 512
TM_OUT = 512
N_SPLIT = 2

F32 = jnp.float32
BF16 = jnp.bfloat16


def _dot(a, b):
    return jnp.dot(a, b, preferred_element_type=F32)


def _dot_nt(a, b):
    return lax.dot_general(a, b, (((1,), (1,)), ((), ())), preferred_element_type=F32)


def _dot_tn(a, b):
    return lax.dot_general(a, b, (((0,), (0,)), ((), ())), preferred_element_type=F32)


def _silu(x):
    return x / (1.0 + jnp.exp2(x * (-LOG2E)))


def _split_bf16(x, parts):
    out = []
    r = x
    for _ in range(parts):
        p = r.astype(BF16)
        out.append(p)
        r = r - p.astype(F32)
    return out


def _lane_iota(shape):
    return lax.broadcasted_iota(jnp.int32, shape, 1)


def _row_iota(shape):
    return lax.broadcasted_iota(jnp.int32, shape, 0)


A_TILE = D_SSD
A_LANE_TILES = A_TILE // LANES


def _in_proj_a_kernel(x0_ref, xnext_ref, nw_ref, w_ref, convw_ref, convb_ref, tail0_ref,
                      pa_ref, xn_out_ref, tail_ref, xn_cur_ref, xn_next_ref, cbuf_ref, *, strip, tiles_per_seq):
    i = pl.program_id(0)
    j = pl.program_id(1)
    tm = xnext_ref.shape[0]

    def norm_rows(src_ref, r0, nrows):
        xv = src_ref[pl.ds(r0, nrows), :]
        ms = jnp.mean(xv * xv, axis=-1, keepdims=True)
        xn_next_ref[pl.ds(r0, nrows), :] = (xv * lax.rsqrt(ms + EPS) * nw_ref[...]).astype(BF16)

    @pl.when((i == 0) & (j == 0))
    def _():
        def body(s, carry):
            norm_rows(x0_ref, pl.multiple_of(s * CHUNK, CHUNK), CHUNK)
            return carry
        lax.fori_loop(0, tm // CHUNK, body, 0)

    @pl.when((i % tiles_per_seq == 0) & (j > 0))
    def _():
        tail_ref[j - 1] = tail0_ref[...]

    def norm_ahead():
        r0 = pl.multiple_of(jnp.minimum(j * strip, tm - strip), CHUNK)
        norm_rows(xnext_ref, r0, strip)

    @pl.when(j == 0)
    def _():
        xn_cur = xn_next_ref[...]
        xn_cur_ref[...] = xn_cur
        xn_out_ref[...] = xn_cur
        norm_ahead()
        pa_ref[...] = _silu(_dot_nt(xn_cur, w_ref[...])).astype(BF16)

    @pl.when(j > 0)
    def _():
        norm_ahead()
        acc = _dot_nt(xn_cur_ref[...], w_ref[...])
        for t in range(A_LANE_TILES):
            cols = slice(t * LANES, (t + 1) * LANES)
            cbuf_ref[t, 0:8, :] = tail_ref[j - 1, t]
            cbuf_ref[t, 8:8 + tm, :] = acc[:, cols]
            conv = convb_ref[:, cols] + convw_ref[0:1, cols] * cbuf_ref[t, 5:5 + tm, :]
            for tap in range(1, CONV_WIDTH):
                conv = conv + convw_ref[tap:tap + 1, cols] * cbuf_ref[t, 5 + tap:5 + tap + tm, :]
            pa_ref[:, cols] = _silu(conv).astype(BF16)
            tail_ref[j - 1, t] = cbuf_ref[t, tm:tm + 8, :]


def _in_proj_a(x2d, norm_w, w_t, conv_w, conv_b, tail0, *, tm, seq):
    rows = x2d.shape[0]
    tn = A_TILE
    nrt, nct = rows // tm, D_PROJ_A // tn
    strip = min(tm, -(-tm // nct // CHUNK) * CHUNK)
    assert strip * nct >= tm and strip <= tm and tm % CHUNK == 0 and seq % tm == 0
    last = nrt - 1
    conv_col = lambda i, j: (0, jnp.maximum(j - 1, 0))
    return pl.pallas_call(
        functools.partial(_in_proj_a_kernel, strip=strip, tiles_per_seq=seq // tm),
        grid=(nrt, nct),
        in_specs=[
            pl.BlockSpec((tm, D_MODEL), lambda i, j: (0, 0)),
            pl.BlockSpec((tm, D_MODEL), lambda i, j: (jnp.minimum(i + 1, last), 0)),
            pl.BlockSpec((1, D_MODEL), lambda i, j: (0, 0)),
            pl.BlockSpec((tn, D_MODEL), lambda i, j: (j, 0)),
            pl.BlockSpec((CONV_WIDTH, tn), conv_col),
            pl.BlockSpec((1, tn), conv_col),
            pl.BlockSpec((A_LANE_TILES, 8, LANES), lambda i, j: (jnp.maximum(j - 1, 0), 0, 0)),
        ],
        out_specs=[
            pl.BlockSpec((tm, tn), lambda i, j: (i, j)),
            pl.BlockSpec((tm, D_MODEL), lambda i, j: (i, 0)),
            pl.BlockSpec((nct - 1, A_LANE_TILES, 8, LANES), lambda i, j: (0, 0, 0, 0)),
        ],
        out_shape=[
            jax.ShapeDtypeStruct((rows, D_PROJ_A), BF16),
            jax.ShapeDtypeStruct((rows, D_MODEL), BF16),
            jax.ShapeDtypeStruct((nct - 1, A_LANE_TILES, 8, LANES), F32),
        ],
        scratch_shapes=[pltpu.VMEM((tm, D_MODEL), BF16),
                        pltpu.VMEM((tm, D_MODEL), BF16),
                        pltpu.VMEM((A_LANE_TILES, tm + 8, LANES), F32)],
        compiler_params=pltpu.CompilerParams(
            dimension_semantics=("arbitrary", "arbitrary"), vmem_limit_bytes=VMEM_LIMIT),
        name="in_proj_a",
    )(x2d, x2d, norm_w, w_t, conv_w, conv_b, tail0)


def _in_proj_b_kernel(xn_ref, wb_ref, wdt_ref, pb_ref, dt_ref):
    xn = xn_ref[...]
    pb_ref[...] = _dot_nt(xn, wb_ref[...]).astype(BF16)
    dt_ref[...] = _dot_nt(xn, wdt_ref[...])


def _in_proj_b(xn, w_b, w_dt, *, tm):
    rows = xn.shape[0]
    return pl.pallas_call(
        _in_proj_b_kernel,
        grid=(rows // tm,),
        in_specs=[
            pl.BlockSpec((tm, D_MODEL), lambda i: (i, 0)),
            pl.BlockSpec((D_PROJ_B, D_MODEL), lambda i: (0, 0)),
            pl.BlockSpec((DT_PAD, D_MODEL), lambda i: (0, 0)),
        ],
        out_specs=[
            pl.BlockSpec((tm, D_PROJ_B), lambda i: (i, 0)),
            pl.BlockSpec((tm, DT_PAD), lambda i: (i, 0)),
        ],
        out_shape=[
            jax.ShapeDtypeStruct((rows, D_PROJ_B), BF16),
            jax.ShapeDtypeStruct((rows, DT_PAD), F32),
        ],
        compiler_params=pltpu.CompilerParams(
            dimension_semantics=("arbitrary",), vmem_limit_bytes=VMEM_LIMIT),
        name="in_proj_b",
    )(xn, w_b, w_dt)


def _dt_cumsum(dt_ref, row0, dtb_ref, alog_ref, valid):
    dtr = dt_ref[pl.ds(row0, CHUNK), :]
    xx = dtr + dtb_ref[...]
    dt = jnp.maximum(xx, 0.0) + jnp.log1p(jnp.exp(-jnp.abs(xx)))
    if valid is not None:
        dt = dt * valid
    dt = jnp.where(_lane_iota((CHUNK, LANES)) < SSD_HEADS, dt, 0.0)
    dta = dt * (-jnp.exp(alog_ref[...]) * LOG2E)
    rr = _row_iota((CHUNK, 3 * CHUNK))
    cc = _lane_iota((CHUNK, 3 * CHUNK)) & (CHUNK - 1)
    tril3 = jnp.where(rr >= cc, 1.0, 0.0).astype(BF16)
    acs = _dot(tril3, jnp.concatenate(_split_bf16(dta, 3), axis=0))
    return dt, acs


def _expand_heads(dt, acs, e_ref):
    a_hi, a_mid, a_lo = [p.astype(F32) for p in _split_bf16(acs, 3)]
    a_stack = (a_hi + pltpu.roll(a_mid, 32, 1) + pltpu.roll(a_lo, 64, 1)).astype(BF16)
    acs_x = _dot(a_stack, e_ref[...])
    d_hi, d_mid = [p.astype(F32) for p in _split_bf16(dt, 2)]
    d_stack = (d_hi + pltpu.roll(d_mid, 32, 1)).astype(BF16)
    dt_x = _dot(d_stack, e_ref[...])
    return dt_x, acs_x


def _state_update(g, xs_ref, bc_ref, rows, dt_x, acs_x, state_ref):
    gl = slice(g * GROUP_W, (g + 1) * GROUP_W)
    acs_g = acs_x[:, gl]
    a_last = acs_g[CHUNK - 1:CHUNK, :]
    x_g = xs_ref[rows, gl].astype(F32)
    x_dt = x_g * dt_x[:, gl]
    xw = (x_dt * jnp.exp2(a_last - acs_g)).astype(BF16)
    b_g = bc_ref[rows, g * D_STATE:(g + 1) * D_STATE]
    state_ref[g] = state_ref[g] * jnp.exp2(a_last) + _dot_tn(b_g, xw)
    return x_dt


def _rope_kv(k_ref, v_ref, row0, ck_ref, sk_ref, kk_ref, vv_ref, slot):
    rows = pl.ds(row0, CHUNK)
    lane = _lane_iota((CHUNK, LANES))
    first_of_pair = (lane & 63) < 32
    low_half = lane < CHUNK
    ck = ck_ref[rows, :]
    sk = sk_ref[rows, :]
    dst = slice(slot * CHUNK, (slot + 1) * CHUNK)
    for c in range(2):
        kx = k_ref[rows, c * LANES:(c + 1) * LANES].astype(F32)
        kr = kx * ck + pltpu.roll(kx, 64, 1) * sk
        kk_ref[2 * c, dst, :] = jnp.where(first_of_pair, kr, pltpu.roll(kr, 32, 1)).astype(BF16)
        kk_ref[2 * c + 1, dst, :] = jnp.where(first_of_pair, pltpu.roll(kr, 96, 1), kr).astype(BF16)
        vx = v_ref[rows, c * LANES:(c + 1) * LANES].astype(F32)
        vr = pltpu.roll(vx, 64, 1)
        vv_ref[2 * c, dst, 0:LANES] = jnp.where(low_half, vx, vr).astype(BF16)
        vv_ref[2 * c + 1, dst, 0:LANES] = jnp.where(low_half, vr, vx).astype(BF16)


def _chunk(refs, row0, chunk_idx):
    (z_ref, xs_ref, bc_ref, q_ref, g_ref, k_ref, v_ref, dt_ref, cq_ref, sq_ref, ck_ref, sk_ref,
     dtb_ref, alog_ref, dskip_ref, nw_ref, e_ref, sinks_ref, out_ref,
     state_ref, xbd_ref, kk_ref, vv_ref) = refs
    rows = pl.ds(row0, CHUNK)
    lane = _lane_iota((CHUNK, LANES))
    row = _row_iota((CHUNK, LANES))
    first_of_pair = (lane & 63) < 32
    low_half = lane < CHUNK
    causal = row >= (lane & (CHUNK - 1))

    dt, acs = _dt_cumsum(dt_ref, row0, dtb_ref, alog_ref, None)
    acs_t = acs.T

    kk_ref[:, 0:2 * CHUNK, :] = kk_ref[:, CHUNK:BAND, :]
    vv_ref[:, 0:2 * CHUNK, 0:LANES] = vv_ref[:, CHUNK:BAND, 0:LANES]
    _rope_kv(k_ref, v_ref, row0, ck_ref, sk_ref, kk_ref, vv_ref, WINDOW_CHUNKS)
    key_abs = (chunk_idx - WINDOW_CHUNKS) * CHUNK + _lane_iota((1, BAND))
    bias = jnp.where(key_abs >= PAD_LEAD, 0.0, NEG_BIG)
    cq = cq_ref[rows, :]
    sq = sq_ref[rows, :]
    expanded = []

    def run(kvs, gs):
        scores = {}
        for kv in kvs:
            q_rows = []
            for c in (2 * kv, 2 * kv + 1):
                qx = q_ref[rows, c * LANES:(c + 1) * LANES].astype(F32)
                qr = qx * cq + pltpu.roll(qx, 64, 1) * sq
                q_rows.append(jnp.where(first_of_pair, qr, 0.0).astype(BF16))
                q_rows.append(jnp.where(first_of_pair, 0.0, qr).astype(BF16))
            q_stack = jnp.concatenate(q_rows, axis=0)
            scores[kv] = _dot_nt(q_stack, kk_ref[kv]) + bias

        cb2, y_off_raw = {}, {}
        for g in gs:
            nl = slice(g * D_STATE, (g + 1) * D_STATE)
            b_g = bc_ref[rows, nl]
            c_g = bc_ref[rows, D_BC // 2 + g * D_STATE:D_BC // 2 + (g + 1) * D_STATE]
            cb2[g] = _dot_nt(c_g, jnp.concatenate([b_g, b_g], axis=0))
            y_off_raw[g] = _dot(c_g, state_ref[g].astype(BF16))

        if not expanded:
            expanded.extend(_expand_heads(dt, acs, e_ref))
        dt_x, acs_x = expanded

        pvs, e_sinks = {}, {}
        for kv in kvs:
            s = scores[kv]
            p_rows, e_sink = [], []
            for r in range(ATT_REP):
                s_r = s[r * CHUNK:(r + 1) * CHUNK, :]
                sink = sinks_ref[kv * ATT_REP + r] * LOG2E
                m = jnp.maximum(jnp.max(s_r, axis=-1, keepdims=True), sink)
                p_rows.append(jnp.exp2(s_r - m).astype(BF16))
                e_sink.append(jnp.exp2(sink - m))
            pvs[kv] = _dot(jnp.concatenate(p_rows, axis=0), vv_ref[kv])
            e_sinks[kv] = e_sink

        y_diag = {}
        for g in gs:
            x_dt = _state_update(g, xs_ref, bc_ref, rows, dt_x, acs_x, state_ref)
            parts = []
            for pp in range(2):
                pls = slice(g * GROUP_W + pp * LANES, g * GROUP_W + (pp + 1) * LANES)
                col = acs_x[:, pls]
                h0 = 2 * (2 * g + pp)
                rowv = jnp.concatenate([acs_t[h0:h0 + 1, :], acs_t[h0 + 1:h0 + 2, :]], axis=1)
                seg = jnp.where(causal, col - rowv, NEG_BIG)
                lmat = (jnp.exp2(seg) * cb2[g]).astype(BF16)
                xp = x_dt[:, pp * LANES:(pp + 1) * LANES].astype(BF16)
                p = 2 * g + pp
                xbd_ref[p, 0:CHUNK, 0:CHUNK] = xp[:, 0:CHUNK]
                xbd_ref[p, CHUNK:, CHUNK:] = xp[:, CHUNK:]
                parts.append(_dot(lmat, xbd_ref[p]))
            y_diag[g] = parts

        for kv in kvs:
            pv = pvs[kv]
            for pr in range(2):
                r0, r1 = 2 * pr, 2 * pr + 1
                blk0 = pv[r0 * CHUNK:(r0 + 1) * CHUNK, :]
                blk1 = pv[r1 * CHUNK:(r1 + 1) * CHUNK, :]
                num = jnp.where(low_half, blk0[:, 0:LANES], blk1[:, 0:LANES])
                den = jnp.where(low_half, blk0[:, LANES:] + e_sinks[kv][r0], blk1[:, LANES:] + e_sinks[kv][r1])
                col = (2 * kv + pr) * LANES
                gate = _silu(g_ref[rows, col:col + LANES].astype(F32))
                out_ref[rows, D_SSD + col:D_SSD + col + LANES] = (num / den * gate).astype(BF16)

        for g in gs:
            gl = slice(g * GROUP_W, (g + 1) * GROUP_W)
            y = (jnp.concatenate(y_diag[g], axis=1) + y_off_raw[g] * jnp.exp2(acs_x[:, gl])
                 + dskip_ref[:, gl] * xs_ref[rows, gl].astype(F32))
            yz = y * z_ref[rows, gl].astype(F32)
            ms = jnp.mean(yz * yz, axis=-1, keepdims=True)
            out_ref[rows, gl] = (yz * lax.rsqrt(ms + EPS) * nw_ref[:, gl]).astype(BF16)

    for h in range(N_SPLIT):
        run(range(h * ATT_KV_HEADS // N_SPLIT, (h + 1) * ATT_KV_HEADS // N_SPLIT),
            range(h * SSD_GROUPS // N_SPLIT, (h + 1) * SSD_GROUPS // N_SPLIT))


def _meta_kernel(xs_ref, bc_ref, k_ref, v_ref, dt_ref, ck_ref, sk_ref,
                 dtb_ref, alog_ref, e_ref,
                 state_ref, kk_ref, vv_ref):
    state_ref[...] = jnp.zeros(state_ref.shape, F32)
    valid = jnp.where(_row_iota((CHUNK, LANES)) >= PAD_LEAD, 1.0, 0.0)
    dt, acs = _dt_cumsum(dt_ref, 0, dtb_ref, alog_ref, valid)
    dt_x, acs_x = _expand_heads(dt, acs, e_ref)
    for g in range(SSD_GROUPS):
        _state_update(g, xs_ref, bc_ref, pl.ds(0, CHUNK), dt_x, acs_x, state_ref)
    _rope_kv(k_ref, v_ref, 0, ck_ref, sk_ref, kk_ref, vv_ref, 0)


def _mixer_kernel(z_ref, xs_ref, bc_ref, q_ref, g_ref, k_ref, v_ref, dt_ref,
                  cq_ref, sq_ref, ck_ref, sk_ref,
                  state0_ref, kk0_ref, vv0_ref,
                  dtb_ref, alog_ref, dskip_ref, nw_ref, e_ref, sinks_ref,
                  out_ref,
                  state_ref, xbd_ref, kk_ref, vv_ref, *, chunks):
    j = pl.program_id(1)

    @pl.when(j == 0)
    def _():
        state_ref[...] = state0_ref[...]
        xbd_ref[...] = jnp.zeros(xbd_ref.shape, BF16)
        kk_ref[...] = jnp.zeros(kk_ref.shape, BF16)
        vv_ref[:, :, 0:LANES] = jnp.zeros((ATT_KV_HEADS, BAND, LANES), BF16)
        vv_ref[:, :, LANES:] = jnp.ones((ATT_KV_HEADS, BAND, LANES), BF16)
        kk_ref[:, 2 * CHUNK:BAND, :] = kk0_ref[...]
        vv_ref[:, 2 * CHUNK:BAND, 0:LANES] = vv0_ref[...]

    refs = (z_ref, xs_ref, bc_ref, q_ref, g_ref, k_ref, v_ref, dt_ref, cq_ref, sq_ref, ck_ref, sk_ref,
            dtb_ref, alog_ref, dskip_ref, nw_ref, e_ref, sinks_ref, out_ref,
            state_ref, xbd_ref, kk_ref, vv_ref)

    def body(ci, carry):
        row0 = pl.multiple_of(ci * CHUNK, CHUNK)
        _chunk(refs, row0, j * chunks + ci + 1)
        return carry

    lax.fori_loop(0, chunks, body, 0)


def _const_spec(shape):
    nd = len(shape)
    return pl.BlockSpec(shape, lambda *_: (0,) * nd)


def _meta_state(projm_a, projm_b, dtm, ckm, skm, dtb, alog, e_mat):
    col = lambda w, idx: pl.BlockSpec((CHUNK, w), lambda i: (0, idx))
    return pl.pallas_call(
        _meta_kernel,
        grid=(1,),
        in_specs=[
            col(D_SSD, 1), col(D_BC, 2), col(D_KV, 2 * D_ATT // D_KV), col(D_KV, 2 * D_ATT // D_KV + 1),
            _const_spec((CHUNK, DT_PAD)), _const_spec((CHUNK, LANES)), _const_spec((CHUNK, LANES)),
            _const_spec((1, DT_PAD)), _const_spec((1, DT_PAD)), _const_spec((LANES, D_SSD)),
        ],
        out_specs=[
            _const_spec((SSD_GROUPS, D_STATE, GROUP_W)),
            _const_spec((ATT_KV_HEADS, CHUNK, LANES)), _const_spec((ATT_KV_HEADS, CHUNK, LANES)),
        ],
        out_shape=[
            jax.ShapeDtypeStruct((SSD_GROUPS, D_STATE, GROUP_W), F32),
            jax.ShapeDtypeStruct((ATT_KV_HEADS, CHUNK, LANES), BF16),
            jax.ShapeDtypeStruct((ATT_KV_HEADS, CHUNK, LANES), BF16),
        ],
        compiler_params=pltpu.CompilerParams(
            dimension_semantics=("arbitrary",), vmem_limit_bytes=VMEM_LIMIT),
        name="meta_state",
    )(projm_a, projm_a, projm_b, projm_b, dtm, ckm, skm, dtb, alog, e_mat)


def _mixer(proj_a, proj_b, dtr, tables, meta_state, params, *, batch, seq, tb):
    cq, sq, ck, sk = tables
    state0, kk0, vv0 = meta_state
    dtb, alog, dskip_x, ssd_nw, e_mat, sinks = params
    nblk = seq // tb
    rows = batch * seq
    chunks = tb // CHUNK
    col = lambda w, idx: pl.BlockSpec((tb, w), lambda b, j: (b * nblk + j, idx))
    tab = pl.BlockSpec((tb, LANES), lambda b, j: (j, 0))
    return pl.pallas_call(
        functools.partial(_mixer_kernel, chunks=chunks),
        grid=(batch, nblk),
        in_specs=[
            col(D_SSD, 0), col(D_SSD, 1), col(D_BC, 2), col(D_ATT, 0), col(D_ATT, 1),
            col(D_KV, 2 * D_ATT // D_KV), col(D_KV, 2 * D_ATT // D_KV + 1), col(DT_PAD, 0),
            tab, tab, tab, tab,
            _const_spec((SSD_GROUPS, D_STATE, GROUP_W)),
            _const_spec((ATT_KV_HEADS, CHUNK, LANES)), _const_spec((ATT_KV_HEADS, CHUNK, LANES)),
            _const_spec((1, DT_PAD)), _const_spec((1, DT_PAD)),
            _const_spec((1, D_SSD)), _const_spec((1, D_SSD)), _const_spec((LANES, D_SSD)),
            pl.BlockSpec(memory_space=pltpu.SMEM),
        ],
        out_specs=pl.BlockSpec((tb, D_MIX), lambda b, j: (b * nblk + j, 0)),
        out_shape=jax.ShapeDtypeStruct((rows, D_MIX), BF16),
        scratch_shapes=[
            pltpu.VMEM((SSD_GROUPS, D_STATE, GROUP_W), F32),
            pltpu.VMEM((SSD_HEADS // 2, 2 * CHUNK, LANES), BF16),
            pltpu.VMEM((ATT_KV_HEADS, BAND, LANES), BF16),
            pltpu.VMEM((ATT_KV_HEADS, BAND, 2 * LANES), BF16),
        ],
        compiler_params=pltpu.CompilerParams(
            dimension_semantics=("arbitrary", "arbitrary"), vmem_limit_bytes=VMEM_LIMIT),
        name="mixer",
    )(proj_a, proj_a, proj_a, proj_b, proj_b, proj_b, proj_b, dtr, cq, sq, ck, sk,
      state0, kk0, vv0, dtb, alog, dskip_x, ssd_nw, e_mat, sinks)


def _out_proj_kernel(mix_ref, w_ref, x_ref, nw_ref, o_ref):
    o = _dot(mix_ref[...], w_ref[...])
    ms = jnp.mean(o * o, axis=-1, keepdims=True)
    o_ref[...] = x_ref[...] + o * lax.rsqrt(ms + EPS) * nw_ref[...]


def _out_proj(mix, w_out, x2d, norm_w, *, tm):
    rows = mix.shape[0]
    return pl.pallas_call(
        _out_proj_kernel,
        grid=(rows // tm,),
        in_specs=[
            pl.BlockSpec((tm, D_MIX), lambda i: (i, 0)),
            pl.BlockSpec((D_MIX, D_MODEL), lambda i: (0, 0)),
            pl.BlockSpec((tm, D_MODEL), lambda i: (i, 0)),
            pl.BlockSpec((1, D_MODEL), lambda i: (0, 0)),
        ],
        out_specs=pl.BlockSpec((tm, D_MODEL), lambda i: (i, 0)),
        out_shape=jax.ShapeDtypeStruct((rows, D_MODEL), F32),
        compiler_params=pltpu.CompilerParams(
            dimension_semantics=("arbitrary",), vmem_limit_bytes=VMEM_LIMIT),
        name="out_proj",
    )(mix, w_out, x2d, norm_w)


def _pair_layout(w_rows, heads):
    half = ATT_HEAD_DIM // 2
    w = w_rows.reshape(heads // 2, 2, 2, half, w_rows.shape[1])
    return w.transpose(0, 2, 1, 3, 4).reshape(heads * ATT_HEAD_DIM, w_rows.shape[1])


def _rope_tables(n_pos):
    half = ATT_HEAD_DIM // 2
    pos = jnp.arange(n_pos, dtype=jnp.int32) - PAD_LEAD
    inv = ROPE_THETA ** (-jnp.arange(half, dtype=F32) / half)
    ang = pos.astype(F32)[:, None] * inv[None, :]
    cos, sin = jnp.cos(ang), jnp.sin(ang)
    cos4 = jnp.concatenate([cos, cos, cos, cos], axis=1)
    sin4 = jnp.concatenate([-sin, -sin, sin, sin], axis=1)
    return cos4, sin4


def kernel(x, meta_tokens, norm_pre_w, w_in, conv_w, conv_b, dt_bias, a_log, d_skip, ssd_norm_w,
           attn_sinks, w_out, norm_post_w):
    batch, seq, _ = x.shape
    assert norm_pre_w.shape[0] == 1 and seq % CHUNK == 0
    rows = batch * seq
    x2d = x.reshape(rows, D_MODEL)

    w_t = jnp.swapaxes(w_in[0], 0, 1).astype(BF16)
    o = D_PROJ_A
    seg = {}
    for name, width in (("dt", SSD_HEADS), ("q", D_ATT), ("k", D_KV), ("v", D_KV), ("g", D_ATT)):
        seg[name] = w_t[o:o + width]
        o += width
    w_b = jnp.concatenate([_pair_layout(seg["q"], ATT_Q_HEADS), seg["g"],
                           _pair_layout(seg["k"], ATT_KV_HEADS), seg["v"]], axis=0)
    w_dt = jnp.pad(seg["dt"], ((0, DT_PAD - SSD_HEADS), (0, 0)))
    w_o = w_out[0].astype(BF16)

    pad_h = lambda v: jnp.pad(v.reshape(1, SSD_HEADS), ((0, 0), (0, DT_PAD - SSD_HEADS)))
    dtb = pad_h(dt_bias[0])
    alog = pad_h(a_log[0])
    dskip_x = jnp.repeat(d_skip[0], SSD_HEAD_DIM).reshape(1, D_SSD)
    e_rows = jnp.arange(LANES)
    e_mat = ((e_rows[:, None] % SSD_HEADS == jnp.arange(D_SSD)[None, :] // SSD_HEAD_DIM)
             & (e_rows[:, None] < 3 * SSD_HEADS)).astype(BF16)

    cos4, sin4 = _rope_tables(CHUNK + seq)
    scale = ATT_HEAD_DIM ** -0.5 * LOG2E
    tables = (cos4[CHUNK:] * scale, sin4[CHUNK:] * scale, cos4[CHUNK:], sin4[CHUNK:])

    npw = norm_pre_w[0].reshape(1, D_MODEL)
    cw, cb = conv_w[0], conv_b[0].reshape(1, D_CONV)

    xm = jnp.concatenate([jnp.zeros((PAD_LEAD, D_MODEL), x.dtype), meta_tokens.astype(x.dtype)], axis=0)
    n_tail = (D_PROJ_A // A_TILE - 1) * A_LANE_TILES
    no_tail = jnp.zeros((n_tail, 8, LANES), F32)
    projm_a, xnm, tail_m = _in_proj_a(xm, npw, w_t, cw, cb, no_tail, tm=CHUNK, seq=CHUNK)
    projm_b, dtm = _in_proj_b(xnm, w_b, w_dt, tm=CHUNK)
    meta_state = _meta_state(projm_a, projm_b, dtm, cos4[:CHUNK], sin4[:CHUNK], dtb, alog, e_mat)

    proj_a, xn, _ = _in_proj_a(x2d, npw, w_t, cw, cb, tail_m.reshape(n_tail, 8, LANES), tm=TM_A, seq=seq)
    proj_b, dtr = _in_proj_b(xn, w_b, w_dt, tm=TM_B)
    params = (dtb, alog, dskip_x, ssd_norm_w[0].reshape(1, D_SSD), e_mat, attn_sinks[0])
    mix = _mixer(proj_a, proj_b, dtr, tables, meta_state, params, batch=batch, seq=seq, tb=TB_MIX)
    out = _out_proj(mix, w_o, x2d, norm_post_w[0].reshape(1, D_MODEL), tm=TM_OUT)
    return out.reshape(batch, seq, D_MODEL)
```

```python
import functools

import jax
import jax.numpy as jnp
from jax import lax
from jax.experimental import pallas as pl
from jax.experimental.pallas import tpu as pltpu

D_MODEL = 2048
CHUNK = 64
N_META = 16
PAD_LEAD = CHUNK - N_META
EPS = 1e-6

SSD_HEADS = 32
SSD_HEAD_DIM = 64
D_SSD = SSD_HEADS * SSD_HEAD_DIM
SSD_GROUPS = 8
D_STATE = 128
CONV_WIDTH = 4
D_BC = 2 * SSD_GROUPS * D_STATE
D_CONV = D_SSD + D_BC
GROUP_W = D_SSD // SSD_GROUPS

ATT_Q_HEADS = 16
ATT_KV_HEADS = 4
ATT_REP = ATT_Q_HEADS // ATT_KV_HEADS
ATT_HEAD_DIM = 64
D_ATT = ATT_Q_HEADS * ATT_HEAD_DIM
D_KV = ATT_KV_HEADS * ATT_HEAD_DIM
WINDOW_CHUNKS = 2
BAND = (WINDOW_CHUNKS + 1) * CHUNK
ROPE_THETA = 10000.0
D_MIX = D_SSD + D_ATT

LANES = 128
DT_PAD = LANES
D_PROJ_A = D_SSD + D_SSD + D_BC
D_PROJ_B = D_ATT + D_ATT + D_KV + D_KV
NEG_BIG = -1e30
LOG2E = 1.4426950408889634
VMEM_LIMIT = 56 * 1024 * 1024
TM_A = 512
TM_B = 1024
TB_MIX = 512
TM_OUT = 512
N_SPLIT = 2

F32 = jnp.float32
BF16 = jnp.bfloat16


def _dot(a, b):
    return jnp.dot(a, b, preferred_element_type=F32)


def _dot_nt(a, b):
    return lax.dot_general(a, b, (((1,), (1,)), ((), ())), preferred_element_type=F32)


def _dot_tn(a, b):
    return lax.dot_general(a, b, (((0,), (0,)), ((), ())), preferred_element_type=F32)


def _silu(x):
    h = 0.5 * x
    return h + h * jnp.tanh(h)


def _split_bf16(x, parts):
    out = []
    r = x
    for _ in range(parts):
        p = r.astype(BF16)
        out.append(p)
        r = r - p.astype(F32)
    return out


def _lane_iota(shape):
    return lax.broadcasted_iota(jnp.int32, shape, 1)


def _row_iota(shape):
    return lax.broadcasted_iota(jnp.int32, shape, 0)


A_TILE = D_SSD
A_LANE_TILES = A_TILE // LANES


def _in_proj_a_kernel(x0_ref, xnext_ref, nw_ref, w_ref, convw_ref, convb_ref, tail0_ref,
                      pa_ref, xn_out_ref, tail_ref, xn_cur_ref, xn_next_ref, cbuf_ref, *, strip, tiles_per_seq):
    i = pl.program_id(0)
    j = pl.program_id(1)
    tm = xnext_ref.shape[0]

    def norm_rows(src_ref, r0, nrows):
        xv = src_ref[pl.ds(r0, nrows), :]
        ms = jnp.mean(xv * xv, axis=-1, keepdims=True)
        xn_next_ref[pl.ds(r0, nrows), :] = (xv * lax.rsqrt(ms + EPS) * nw_ref[...]).astype(BF16)

    @pl.when((i == 0) & (j == 0))
    def _():
        def body(s, carry):
            norm_rows(x0_ref, pl.multiple_of(s * CHUNK, CHUNK), CHUNK)
            return carry
        lax.fori_loop(0, tm // CHUNK, body, 0)

    @pl.when((i % tiles_per_seq == 0) & (j > 0))
    def _():
        tail_ref[j - 1] = tail0_ref[...]

    def norm_ahead():
        r0 = pl.multiple_of(jnp.minimum(j * strip, tm - strip), CHUNK)
        norm_rows(xnext_ref, r0, strip)

    @pl.when(j == 0)
    def _():
        xn_cur = xn_next_ref[...]
        xn_cur_ref[...] = xn_cur
        xn_out_ref[...] = xn_cur
        norm_ahead()
        pa_ref[...] = _silu(_dot_nt(xn_cur, w_ref[...])).astype(BF16)

    @pl.when(j > 0)
    def _():
        norm_ahead()
        acc = _dot_nt(xn_cur_ref[...], w_ref[...])
        for t in range(A_LANE_TILES):
            cols = slice(t * LANES, (t + 1) * LANES)
            cbuf_ref[t, 0:8, :] = tail_ref[j - 1, t]
            cbuf_ref[t, 8:8 + tm, :] = acc[:, cols]
            conv = convb_ref[:, cols] + convw_ref[0:1, cols] * cbuf_ref[t, 5:5 + tm, :]
            for tap in range(1, CONV_WIDTH):
                conv = conv + convw_ref[tap:tap + 1, cols] * cbuf_ref[t, 5 + tap:5 + tap + tm, :]
            pa_ref[:, cols] = _silu(conv).astype(BF16)
            tail_ref[j - 1, t] = cbuf_ref[t, tm:tm + 8, :]


def _in_proj_a(x2d, norm_w, w_t, conv_w, conv_b, tail0, *, tm, seq):
    rows = x2d.shape[0]
    tn = A_TILE
    nrt, nct = rows // tm, D_PROJ_A // tn
    strip = min(tm, -(-tm // nct // CHUNK) * CHUNK)
    assert strip * nct >= tm and strip <= tm and tm % CHUNK == 0 and seq % tm == 0
    last = nrt - 1
    conv_col = lambda i, j: (0, jnp.maximum(j - 1, 0))
    return pl.pallas_call(
        functools.partial(_in_proj_a_kernel, strip=strip, tiles_per_seq=seq // tm),
        grid=(nrt, nct),
        in_specs=[
            pl.BlockSpec((tm, D_MODEL), lambda i, j: (0, 0)),
            pl.BlockSpec((tm, D_MODEL), lambda i, j: (jnp.minimum(i + 1, last), 0)),
            pl.BlockSpec((1, D_MODEL), lambda i, j: (0, 0)),
            pl.BlockSpec((tn, D_MODEL), lambda i, j: (j, 0)),
            pl.BlockSpec((CONV_WIDTH, tn), conv_col),
            pl.BlockSpec((1, tn), conv_col),
            pl.BlockSpec((A_LANE_TILES, 8, LANES), lambda i, j: (jnp.maximum(j - 1, 0), 0, 0)),
        ],
        out_specs=[
            pl.BlockSpec((tm, tn), lambda i, j: (i, j)),
            pl.BlockSpec((tm, D_MODEL), lambda i, j: (i, 0)),
            pl.BlockSpec((nct - 1, A_LANE_TILES, 8, LANES), lambda i, j: (0, 0, 0, 0)),
        ],
        out_shape=[
            jax.ShapeDtypeStruct((rows, D_PROJ_A), BF16),
            jax.ShapeDtypeStruct((rows, D_MODEL), BF16),
            jax.ShapeDtypeStruct((nct - 1, A_LANE_TILES, 8, LANES), F32),
        ],
        scratch_shapes=[pltpu.VMEM((tm, D_MODEL), BF16),
                        pltpu.VMEM((tm, D_MODEL), BF16),
                        pltpu.VMEM((A_LANE_TILES, tm + 8, LANES), F32)],
        compiler_params=pltpu.CompilerParams(
            dimension_semantics=("arbitrary", "arbitrary"), vmem_limit_bytes=VMEM_LIMIT),
        name="in_proj_a",
    )(x2d, x2d, norm_w, w_t, conv_w, conv_b, tail0)


def _in_proj_b_kernel(xn_ref, wb_ref, wdt_ref, pb_ref, dt_ref):
    xn = xn_ref[...]
    pb_ref[...] = _dot_nt(xn, wb_ref[...]).astype(BF16)
    dt_ref[...] = _dot_nt(xn, wdt_ref[...])


def _in_proj_b(xn, w_b, w_dt, *, tm):
    rows = xn.shape[0]
    return pl.pallas_call(
        _in_proj_b_kernel,
        grid=(rows // tm,),
        in_specs=[
            pl.BlockSpec((tm, D_MODEL), lambda i: (i, 0)),
            pl.BlockSpec((D_PROJ_B, D_MODEL), lambda i: (0, 0)),
            pl.BlockSpec((DT_PAD, D_MODEL), lambda i: (0, 0)),
        ],
        out_specs=[
            pl.BlockSpec((tm, D_PROJ_B), lambda i: (i, 0)),
            pl.BlockSpec((tm, DT_PAD), lambda i: (i, 0)),
        ],
        out_shape=[
            jax.ShapeDtypeStruct((rows, D_PROJ_B), BF16),
            jax.ShapeDtypeStruct((rows, DT_PAD), F32),
        ],
        compiler_params=pltpu.CompilerParams(
            dimension_semantics=("arbitrary",), vmem_limit_bytes=VMEM_LIMIT),
        name="in_proj_b",
    )(xn, w_b, w_dt)


def _dt_cumsum(dt_ref, row0, dtb_ref, alog_ref, valid):
    dtr = dt_ref[pl.ds(row0, CHUNK), :]
    xx = dtr + dtb_ref[...]
    dt = jnp.maximum(xx, 0.0) + jnp.log1p(jnp.exp(-jnp.abs(xx)))
    if valid is not None:
        dt = dt * valid
    dt = jnp.where(_lane_iota((CHUNK, LANES)) < SSD_HEADS, dt, 0.0)
    dta = dt * (-jnp.exp(alog_ref[...]) * LOG2E)
    rr = _row_iota((CHUNK, 3 * CHUNK))
    cc = _lane_iota((CHUNK, 3 * CHUNK)) & (CHUNK - 1)
    tril3 = jnp.where(rr >= cc, 1.0, 0.0).astype(BF16)
    acs = _dot(tril3, jnp.concatenate(_split_bf16(dta, 3), axis=0))
    return dt, acs


def _expand_heads(dt, acs, e_ref):
    a_hi, a_mid, a_lo = [p.astype(F32) for p in _split_bf16(acs, 3)]
    a_stack = (a_hi + pltpu.roll(a_mid, 32, 1) + pltpu.roll(a_lo, 64, 1)).astype(BF16)
    acs_x = _dot(a_stack, e_ref[...])
    d_hi, d_mid = [p.astype(F32) for p in _split_bf16(dt, 2)]
    d_stack = (d_hi + pltpu.roll(d_mid, 32, 1)).astype(BF16)
    dt_x = _dot(d_stack, e_ref[...])
    return dt_x, acs_x


def _state_update(g, xs_ref, bc_ref, rows, dt_x, acs_x, state_ref):
    gl = slice(g * GROUP_W, (g + 1) * GROUP_W)
    acs_g = acs_x[:, gl]
    a_last = acs_g[CHUNK - 1:CHUNK, :]
    x_g = xs_ref[rows, gl].astype(F32)
    x_dt = x_g * dt_x[:, gl]
    xw = (x_dt * jnp.exp2(a_last - acs_g)).astype(BF16)
    b_g = bc_ref[rows, g * D_STATE:(g + 1) * D_STATE]
    state_ref[g] = state_ref[g] * jnp.exp2(a_last) + _dot_tn(b_g, xw)
    return x_dt


def _rope_kv(k_ref, v_ref, row0, ck_ref, sk_ref, kk_ref, vv_ref, slot):
    rows = pl.ds(row0, CHUNK)
    lane = _lane_iota((CHUNK, LANES))
    first_of_pair = (lane & 63) < 32
    low_half = lane < CHUNK
    ck = ck_ref[rows, :]
    sk = sk_ref[rows, :]
    dst = slice(slot * CHUNK, (slot + 1) * CHUNK)
    for c in range(2):
        kx = k_ref[rows, c * LANES:(c + 1) * LANES].astype(F32)
        kr = kx * ck + pltpu.roll(kx, 64, 1) * sk
        kk_ref[2 * c, dst, :] = jnp.where(first_of_pair, kr, pltpu.roll(kr, 32, 1)).astype(BF16)
        kk_ref[2 * c + 1, dst, :] = jnp.where(first_of_pair, pltpu.roll(kr, 96, 1), kr).astype(BF16)
        vx = v_ref[rows, c * LANES:(c + 1) * LANES].astype(F32)
        vr = pltpu.roll(vx, 64, 1)
        vv_ref[2 * c, dst, 0:LANES] = jnp.where(low_half, vx, vr).astype(BF16)
        vv_ref[2 * c + 1, dst, 0:LANES] = jnp.where(low_half, vr, vx).astype(BF16)


def _chunk(refs, row0, chunk_idx):
    (z_ref, xs_ref, bc_ref, q_ref, g_ref, k_ref, v_ref, dt_ref, cq_ref, sq_ref, ck_ref, sk_ref,
     dtb_ref, alog_ref, dskip_ref, nw_ref, e_ref, sinks_ref, out_ref,
     state_ref, xbd_ref, kk_ref, vv_ref) = refs
    rows = pl.ds(row0, CHUNK)
    lane = _lane_iota((CHUNK, LANES))
    row = _row_iota((CHUNK, LANES))
    first_of_pair = (lane & 63) < 32
    low_half = lane < CHUNK
    causal = row >= (lane & (CHUNK - 1))

    dt, acs = _dt_cumsum(dt_ref, row0, dtb_ref, alog_ref, None)
    acs_t = acs.T

    kk_ref[:, 0:2 * CHUNK, :] = kk_ref[:, CHUNK:BAND, :]
    vv_ref[:, 0:2 * CHUNK, 0:LANES] = vv_ref[:, CHUNK:BAND, 0:LANES]
    _rope_kv(k_ref, v_ref, row0, ck_ref, sk_ref, kk_ref, vv_ref, WINDOW_CHUNKS)
    key_abs = (chunk_idx - WINDOW_CHUNKS) * CHUNK + _lane_iota((1, BAND))
    bias = jnp.where(key_abs >= PAD_LEAD, 0.0, NEG_BIG)
    cq = cq_ref[rows, :]
    sq = sq_ref[rows, :]
    expanded = []

    def run(kvs, gs):
        scores = {}
        for kv in kvs:
            q_rows = []
            for c in (2 * kv, 2 * kv + 1):
                qx = q_ref[rows, c * LANES:(c + 1) * LANES].astype(F32)
                qr = qx * cq + pltpu.roll(qx, 64, 1) * sq
                q_rows.append(jnp.where(first_of_pair, qr, 0.0).astype(BF16))
                q_rows.append(jnp.where(first_of_pair, 0.0, qr).astype(BF16))
            q_stack = jnp.concatenate(q_rows, axis=0)
            scores[kv] = _dot_nt(q_stack, kk_ref[kv]) + bias

        cb2, y_off_raw = {}, {}
        for g in gs:
            nl = slice(g * D_STATE, (g + 1) * D_STATE)
            b_g = bc_ref[rows, nl]
            c_g = bc_ref[rows, D_BC // 2 + g * D_STATE:D_BC // 2 + (g + 1) * D_STATE]
            cb2[g] = _dot_nt(c_g, jnp.concatenate([b_g, b_g], axis=0))
            y_off_raw[g] = _dot(c_g, state_ref[g].astype(BF16))

        if not expanded:
            expanded.extend(_expand_heads(dt, acs, e_ref))
        dt_x, acs_x = expanded

        pvs, e_sinks = {}, {}
        for kv in kvs:
            s = scores[kv]
            p_rows, e_sink = [], []
            for r in range(ATT_REP):
                s_r = s[r * CHUNK:(r + 1) * CHUNK, :]
                sink = sinks_ref[kv * ATT_REP + r] * LOG2E
                m = jnp.maximum(jnp.max(s_r, axis=-1, keepdims=True), sink)
                p_rows.append(jnp.exp2(s_r - m).astype(BF16))
                e_sink.append(jnp.exp2(sink - m))
            pvs[kv] = _dot(jnp.concatenate(p_rows, axis=0), vv_ref[kv])
            e_sinks[kv] = e_sink

        y_diag = {}
        for g in gs:
            x_dt = _state_update(g, xs_ref, bc_ref, rows, dt_x, acs_x, state_ref)
            parts = []
            for pp in range(2):
                pls = slice(g * GROUP_W + pp * LANES, g * GROUP_W + (pp + 1) * LANES)
                col = acs_x[:, pls]
                h0 = 2 * (2 * g + pp)
                rowv = jnp.concatenate([acs_t[h0:h0 + 1, :], acs_t[h0 + 1:h0 + 2, :]], axis=1)
                seg = jnp.where(causal, col - rowv, NEG_BIG)
                lmat = (jnp.exp2(seg) * cb2[g]).astype(BF16)
                xp = x_dt[:, pp * LANES:(pp + 1) * LANES].astype(BF16)
                p = 2 * g + pp
                xbd_ref[p, 0:CHUNK, 0:CHUNK] = xp[:, 0:CHUNK]
                xbd_ref[p, CHUNK:, CHUNK:] = xp[:, CHUNK:]
                parts.append(_dot(lmat, xbd_ref[p]))
            y_diag[g] = parts

        for kv in kvs:
            pv = pvs[kv]
            for pr in range(2):
                r0, r1 = 2 * pr, 2 * pr + 1
                blk0 = pv[r0 * CHUNK:(r0 + 1) * CHUNK, :]
                blk1 = pv[r1 * CHUNK:(r1 + 1) * CHUNK, :]
                num = jnp.where(low_half, blk0[:, 0:LANES], blk1[:, 0:LANES])
                den = jnp.where(low_half, blk0[:, LANES:] + e_sinks[kv][r0], blk1[:, LANES:] + e_sinks[kv][r1])
                col = (2 * kv + pr) * LANES
                gate = _silu(g_ref[rows, col:col + LANES].astype(F32))
                out_ref[rows, D_SSD + col:D_SSD + col + LANES] = (num / den * gate).astype(BF16)

        for g in gs:
            gl = slice(g * GROUP_W, (g + 1) * GROUP_W)
            y = (jnp.concatenate(y_diag[g], axis=1) + y_off_raw[g] * jnp.exp2(acs_x[:, gl])
                 + dskip_ref[:, gl] * xs_ref[rows, gl].astype(F32))
            yz = y * z_ref[rows, gl].astype(F32)
            ms = jnp.mean(yz * yz, axis=-1, keepdims=True)
            out_ref[rows, gl] = (yz * lax.rsqrt(ms + EPS) * nw_ref[:, gl]).astype(BF16)

    for h in range(N_SPLIT):
        run(range(h * ATT_KV_HEADS // N_SPLIT, (h + 1) * ATT_KV_HEADS // N_SPLIT),
            range(h * SSD_GROUPS // N_SPLIT, (h + 1) * SSD_GROUPS // N_SPLIT))


def _meta_kernel(xs_ref, bc_ref, k_ref, v_ref, dt_ref, ck_ref, sk_ref,
                 dtb_ref, alog_ref, e_ref,
                 state_ref, kk_ref, vv_ref):
    state_ref[...] = jnp.zeros(state_ref.shape, F32)
    valid = jnp.where(_row_iota((CHUNK, LANES)) >= PAD_LEAD, 1.0, 0.0)
    dt, acs = _dt_cumsum(dt_ref, 0, dtb_ref, alog_ref, valid)
    dt_x, acs_x = _expand_heads(dt, acs, e_ref)
    for g in range(SSD_GROUPS):
        _state_update(g, xs_ref, bc_ref, pl.ds(0, CHUNK), dt_x, acs_x, state_ref)
    _rope_kv(k_ref, v_ref, 0, ck_ref, sk_ref, kk_ref, vv_ref, 0)


def _mixer_kernel(z_ref, xs_ref, bc_ref, q_ref, g_ref, k_ref, v_ref, dt_ref,
                  cq_ref, sq_ref, ck_ref, sk_ref,
                  state0_ref, kk0_ref, vv0_ref,
                  dtb_ref, alog_ref, dskip_ref, nw_ref, e_ref, sinks_ref,
                  out_ref,
                  state_ref, xbd_ref, kk_ref, vv_ref, *, chunks):
    j = pl.program_id(1)

    @pl.when(j == 0)
    def _():
        state_ref[...] = state0_ref[...]
        xbd_ref[...] = jnp.zeros(xbd_ref.shape, BF16)
        kk_ref[...] = jnp.zeros(kk_ref.shape, BF16)
        vv_ref[:, :, 0:LANES] = jnp.zeros((ATT_KV_HEADS, BAND, LANES), BF16)
        vv_ref[:, :, LANES:] = jnp.ones((ATT_KV_HEADS, BAND, LANES), BF16)
        kk_ref[:, 2 * CHUNK:BAND, :] = kk0_ref[...]
        vv_ref[:, 2 * CHUNK:BAND, 0:LANES] = vv0_ref[...]

    refs = (z_ref, xs_ref, bc_ref, q_ref, g_ref, k_ref, v_ref, dt_ref, cq_ref, sq_ref, ck_ref, sk_ref,
            dtb_ref, alog_ref, dskip_ref, nw_ref, e_ref, sinks_ref, out_ref,
            state_ref, xbd_ref, kk_ref, vv_ref)

    def body(ci, carry):
        row0 = pl.multiple_of(ci * CHUNK, CHUNK)
        _chunk(refs, row0, j * chunks + ci + 1)
        return carry

    lax.fori_loop(0, chunks, body, 0)


def _const_spec(shape):
    nd = len(shape)
    return pl.BlockSpec(shape, lambda *_: (0,) * nd)


def _meta_state(projm_a, projm_b, dtm, ckm, skm, dtb, alog, e_mat):
    col = lambda w, idx: pl.BlockSpec((CHUNK, w), lambda i: (0, idx))
    return pl.pallas_call(
        _meta_kernel,
        grid=(1,),
        in_specs=[
            col(D_SSD, 1), col(D_BC, 2), col(D_KV, 2 * D_ATT // D_KV), col(D_KV, 2 * D_ATT // D_KV + 1),
            _const_spec((CHUNK, DT_PAD)), _const_spec((CHUNK, LANES)), _const_spec((CHUNK, LANES)),
            _const_spec((1, DT_PAD)), _const_spec((1, DT_PAD)), _const_spec((LANES, D_SSD)),
        ],
        out_specs=[
            _const_spec((SSD_GROUPS, D_STATE, GROUP_W)),
            _const_spec((ATT_KV_HEADS, CHUNK, LANES)), _const_spec((ATT_KV_HEADS, CHUNK, LANES)),
        ],
        out_shape=[
            jax.ShapeDtypeStruct((SSD_GROUPS, D_STATE, GROUP_W), F32),
            jax.ShapeDtypeStruct((ATT_KV_HEADS, CHUNK, LANES), BF16),
            jax.ShapeDtypeStruct((ATT_KV_HEADS, CHUNK, LANES), BF16),
        ],
        compiler_params=pltpu.CompilerParams(
            dimension_semantics=("arbitrary",), vmem_limit_bytes=VMEM_LIMIT),
        name="meta_state",
    )(projm_a, projm_a, projm_b, projm_b, dtm, ckm, skm, dtb, alog, e_mat)


def _mixer(proj_a, proj_b, dtr, tables, meta_state, params, *, batch, seq, tb):
    cq, sq, ck, sk = tables
    state0, kk0, vv0 = meta_state
    dtb, alog, dskip_x, ssd_nw, e_mat, sinks = params
    nblk = seq // tb
    rows = batch * seq
    chunks = tb // CHUNK
    col = lambda w, idx: pl.BlockSpec((tb, w), lambda b, j: (b * nblk + j, idx))
    tab = pl.BlockSpec((tb, LANES), lambda b, j: (j, 0))
    return pl.pallas_call(
        functools.partial(_mixer_kernel, chunks=chunks),
        grid=(batch, nblk),
        in_specs=[
            col(D_SSD, 0), col(D_SSD, 1), col(D_BC, 2), col(D_ATT, 0), col(D_ATT, 1),
            col(D_KV, 2 * D_ATT // D_KV), col(D_KV, 2 * D_ATT // D_KV + 1), col(DT_PAD, 0),
            tab, tab, tab, tab,
            _const_spec((SSD_GROUPS, D_STATE, GROUP_W)),
            _const_spec((ATT_KV_HEADS, CHUNK, LANES)), _const_spec((ATT_KV_HEADS, CHUNK, LANES)),
            _const_spec((1, DT_PAD)), _const_spec((1, DT_PAD)),
            _const_spec((1, D_SSD)), _const_spec((1, D_SSD)), _const_spec((LANES, D_SSD)),
            pl.BlockSpec(memory_space=pltpu.SMEM),
        ],
        out_specs=pl.BlockSpec((tb, D_MIX), lambda b, j: (b * nblk + j, 0)),
        out_shape=jax.ShapeDtypeStruct((rows, D_MIX), BF16),
        scratch_shapes=[
            pltpu.VMEM((SSD_GROUPS, D_STATE, GROUP_W), F32),
            pltpu.VMEM((SSD_HEADS // 2, 2 * CHUNK, LANES), BF16),
            pltpu.VMEM((ATT_KV_HEADS, BAND, LANES), BF16),
            pltpu.VMEM((ATT_KV_HEADS, BAND, 2 * LANES), BF16),
        ],
        compiler_params=pltpu.CompilerParams(
            dimension_semantics=("arbitrary", "arbitrary"), vmem_limit_bytes=VMEM_LIMIT),
        name="mixer",
    )(proj_a, proj_a, proj_a, proj_b, proj_b, proj_b, proj_b, dtr, cq, sq, ck, sk,
      state0, kk0, vv0, dtb, alog, dskip_x, ssd_nw, e_mat, sinks)


def _out_proj_kernel(mix_ref, w_ref, x_ref, nw_ref, o_ref):
    o = _dot(mix_ref[...], w_ref[...])
    ms = jnp.mean(o * o, axis=-1, keepdims=True)
    o_ref[...] = x_ref[...] + o * lax.rsqrt(ms + EPS) * nw_ref[...]


def _out_proj(mix, w_out, x2d, norm_w, *, tm):
    rows = mix.shape[0]
    return pl.pallas_call(
        _out_proj_kernel,
        grid=(rows // tm,),
        in_specs=[
            pl.BlockSpec((tm, D_MIX), lambda i: (i, 0)),
            pl.BlockSpec((D_MIX, D_MODEL), lambda i: (0, 0)),
            pl.BlockSpec((tm, D_MODEL), lambda i: (i, 0)),
            pl.BlockSpec((1, D_MODEL), lambda i: (0, 0)),
        ],
        out_specs=pl.BlockSpec((tm, D_MODEL), lambda i: (i, 0)),
        out_shape=jax.ShapeDtypeStruct((rows, D_MODEL), F32),
        compiler_params=pltpu.CompilerParams(
            dimension_semantics=("arbitrary",), vmem_limit_bytes=VMEM_LIMIT),
        name="out_proj",
    )(mix, w_out, x2d, norm_w)


def _pair_layout(w_rows, heads):
    half = ATT_HEAD_DIM // 2
    w = w_rows.reshape(heads // 2, 2, 2, half, w_rows.shape[1])
    return w.transpose(0, 2, 1, 3, 4).reshape(heads * ATT_HEAD_DIM, w_rows.shape[1])


def _rope_tables(n_pos):
    half = ATT_HEAD_DIM // 2
    pos = jnp.arange(n_pos, dtype=jnp.int32) - PAD_LEAD
    inv = ROPE_THETA ** (-jnp.arange(half, dtype=F32) / half)
    ang = pos.astype(F32)[:, None] * inv[None, :]
    cos, sin = jnp.cos(ang), jnp.sin(ang)
    cos4 = jnp.concatenate([cos, cos, cos, cos], axis=1)
    sin4 = jnp.concatenate([-sin, -sin, sin, sin], axis=1)
    return cos4, sin4


def kernel(x, meta_tokens, norm_pre_w, w_in, conv_w, conv_b, dt_bias, a_log, d_skip, ssd_norm_w,
           attn_sinks, w_out, norm_post_w):
    batch, seq, _ = x.shape
    assert norm_pre_w.shape[0] == 1 and seq % CHUNK == 0
    rows = batch * seq
    x2d = x.reshape(rows, D_MODEL)

    w_t = jnp.swapaxes(w_in[0], 0, 1).astype(BF16)
    o = D_PROJ_A
    seg = {}
    for name, width in (("dt", SSD_HEADS), ("q", D_ATT), ("k", D_KV), ("v", D_KV), ("g", D_ATT)):
        seg[name] = w_t[o:o + width]
        o += width
    w_b = jnp.concatenate([_pair_layout(seg["q"], ATT_Q_HEADS), seg["g"],
                           _pair_layout(seg["k"], ATT_KV_HEADS), seg["v"]], axis=0)
    w_dt = jnp.pad(seg["dt"], ((0, DT_PAD - SSD_HEADS), (0, 0)))
    w_o = w_out[0].astype(BF16)

    pad_h = lambda v: jnp.pad(v.reshape(1, SSD_HEADS), ((0, 0), (0, DT_PAD - SSD_HEADS)))
    dtb = pad_h(dt_bias[0])
    alog = pad_h(a_log[0])
    dskip_x = jnp.repeat(d_skip[0], SSD_HEAD_DIM).reshape(1, D_SSD)
    e_rows = jnp.arange(LANES)
    e_mat = ((e_rows[:, None] % SSD_HEADS == jnp.arange(D_SSD)[None, :] // SSD_HEAD_DIM)
             & (e_rows[:, None] < 3 * SSD_HEADS)).astype(BF16)

    cos4, sin4 = _rope_tables(CHUNK + seq)
    scale = ATT_HEAD_DIM ** -0.5 * LOG2E
    tables = (cos4[CHUNK:] * scale, sin4[CHUNK:] * scale, cos4[CHUNK:], sin4[CHUNK:])

    npw = norm_pre_w[0].reshape(1, D_MODEL)
    cw, cb = conv_w[0], conv_b[0].reshape(1, D_CONV)

    xm = jnp.concatenate([jnp.zeros((PAD_LEAD, D_MODEL), x.dtype), meta_tokens.astype(x.dtype)], axis=0)
    n_tail = (D_PROJ_A // A_TILE - 1) * A_LANE_TILES
    no_tail = jnp.zeros((n_tail, 8, LANES), F32)
    projm_a, xnm, tail_m = _in_proj_a(xm, npw, w_t, cw, cb, no_tail, tm=CHUNK, seq=CHUNK)
    projm_b, dtm = _in_proj_b(xnm, w_b, w_dt, tm=CHUNK)
    meta_state = _meta_state(projm_a, projm_b, dtm, cos4[:CHUNK], sin4[:CHUNK], dtb, alog, e_mat)

    proj_a, xn, _ = _in_proj_a(x2d, npw, w_t, cw, cb, tail_m.reshape(n_tail, 8, LANES), tm=TM_A, seq=seq)
    proj_b, dtr = _in_proj_b(xn, w_b, w_dt, tm=TM_B)
    params = (dtb, alog, dskip_x, ssd_norm_w[0].reshape(1, D_SSD), e_mat, attn_sinks[0])
    mix = _mixer(proj_a, proj_b, dtr, tables, meta_state, params, batch=batch, seq=seq, tb=TB_MIX)
    out = _out_proj(mix, w_o, x2d, norm_post_w[0].reshape(1, D_MODEL), tm=TM_OUT)
    return out.reshape(batch, seq, D_MODEL)
```
